```python
import jax
import jax.numpy as jnp
from jax import lax
import numpy as np

D_MODEL = 2048
BATCH = 4
SEQ = 2048
DEPTH = 4

GRID_W = 64
CTX_LEN = 256
N_MIXERS = 3
EXPAND = 2
D_INNER = EXPAND * D_MODEL
HEAD_DIM = 128
N_HEADS = D_INNER // HEAD_DIM
WIN_H = 8
WIN_W = 16
RPB_H = 2 * WIN_H - 1
RPB_W = 2 * WIN_W - 1
POOL_SIZES = (2, 4, 8, 16)
N_POOL_GROUPS = len(POOL_SIZES)
POOL_GROUP = D_INNER // N_POOL_GROUPS
HGRN_HEADS = D_INNER // HEAD_DIM
HGRN_KDIM = 128
FORGET_DIM = HGRN_HEADS * HGRN_KDIM
CHUNK = 32
N_NA = (DEPTH + 2) // 3
N_POOL = (DEPTH + 1) // 3
N_HGRN = DEPTH // 3
EPS = 1e-6
MOD_STD = 0.5

kernel_name = 'hybrid_na_pool_hgrn2_dit_trunk'


def rmsnorm(x, g):
    xf = x.astype(jnp.float32)
    y = xf * lax.rsqrt(jnp.mean(xf * xf, axis=-1, keepdims=True) + EPS)
    return (y * g.astype(jnp.float32)).astype(x.dtype)


def _heads(t):
    b, n, _ = t.shape
    return t.reshape(b, n, -1, HEAD_DIM).transpose(0, 2, 1, 3)


def na_mixer(h, hc, w_in, rpb, w_out, need_ctx):
    B, T, _ = h.shape
    rows = T // GRID_W
    wh = min(WIN_H, rows)
    q, k, v, z = jnp.split(h @ w_in, 4, axis=-1)
    qc, kc, vc, zc = jnp.split(hc @ w_in, 4, axis=-1)
    sc = HEAD_DIM ** -0.5
    grid = (B, N_HEADS, rows, GRID_W, HEAD_DIM)
    qg = (_heads(q) * sc).reshape(grid)
    kg = _heads(k).reshape(grid)
    vg = _heads(v).reshape(grid)
    kch, vch = _heads(kc), _heads(vc)
    col = jnp.arange(GRID_W)
    c0 = jnp.clip(col - WIN_W // 2, 0, GRID_W - WIN_W)
    dcol = col[None, :] - col[:, None]
    col_in = (col[None, :] >= c0[:, None]) & (col[None, :] < c0[:, None] + WIN_W)
    rpb_c = rpb[:, :, jnp.clip(dcol, -(WIN_W - 1), WIN_W - 1) + (WIN_W - 1)]
    n_lat = wh * GRID_W

    def row_block(args):
        r, q_r = args
        r0 = jnp.clip(r - wh // 2, 0, rows - wh)
        k_b = lax.dynamic_slice_in_dim(kg, r0, wh, axis=2)
        v_b = lax.dynamic_slice_in_dim(vg, r0, wh, axis=2)
        drow = r0 + jnp.arange(wh) - r + (WIN_H - 1)
        bias = jnp.take(rpb_c, drow, axis=1).transpose(0, 2, 1, 3)
        s_lat = jnp.einsum('bhqd,bhrkd->bhqrk', q_r, k_b).astype(jnp.float32) + bias.astype(jnp.float32)
        s_lat = jnp.where(col_in[:, None, :], s_lat, -jnp.inf).reshape(B, N_HEADS, GRID_W, n_lat)
        s_ctx = jnp.einsum('bhqd,bhmd->bhqm', q_r, kch).astype(jnp.float32)
        p = jax.nn.softmax(jnp.concatenate([s_lat, s_ctx], axis=-1), axis=-1).astype(v_b.dtype)
        p_lat = p[..., :n_lat].reshape(B, N_HEADS, GRID_W, wh, GRID_W)
        return (jnp.einsum('bhqrk,bhrkd->bhqd', p_lat, v_b)
                + jnp.einsum('bhqm,bhmd->bhqd', p[..., n_lat:], vch))

    o = lax.map(row_block, (jnp.arange(rows), qg.transpose(2, 0, 1, 3, 4)))
    o = o.transpose(1, 0, 3, 2, 4).reshape(B, T, D_INNER)
    y = (o * jax.nn.silu(z)) @ w_out
    yc = None
    if need_ctx:
        qch = _heads(qc) * sc
        pc = jax.nn.softmax(jnp.einsum('bhqd,bhkd->bhqk', qch, kch).astype(jnp.float32), axis=-1)
        oc = jnp.einsum('bhqk,bhkd->bhqd', pc.astype(vch.dtype), vch)
        oc = oc.transpose(0, 2, 1, 3).reshape(B, -1, D_INNER)
        yc = (oc * jax.nn.silu(zc)) @ w_out
    return y, yc


def centred_mean(u, w):
    T = u.shape[1]
    cs = jnp.cumsum(u.astype(jnp.float32), axis=1)
    cs = jnp.pad(cs, ((0, 0), (1, 0), (0, 0)))
    t = jnp.arange(T)
    lo = jnp.clip(t - w // 2, 0, T)
    hi = jnp.clip(t + w - w // 2, 0, T)
    cnt = (hi - lo).astype(jnp.float32)
    return ((cs[:, hi] - cs[:, lo]) / cnt[None, :, None]).astype(u.dtype)


def _pool_branch(t, w_in, w_grp, scale, w_out):
    u, z = jnp.split(t @ w_in, 2, axis=-1)
    ug = u.reshape(u.shape[0], u.shape[1], N_POOL_GROUPS, POOL_GROUP)
    pooled = jnp.stack([centred_mean(ug[:, :, g, :], POOL_SIZES[g]) for g in range(N_POOL_GROUPS)], axis=2) - ug
    y = jnp.einsum('btgc,gce->btge', pooled, w_grp).reshape(u.shape) * scale
    return (y * jax.nn.silu(z)) @ w_out


def pool_mixer(h, hc, w_in, w_grp, scale, w_out, need_ctx):
    y = _pool_branch(h, w_in, w_grp, scale, w_out)
    yc = _pool_branch(hc, w_in, w_grp, scale, w_out) if need_ctx else None
    return y, yc


def hgrn_scan(q, k, v, logf, s0):
    B, H, N, _ = q.shape
    dv = v.shape[-1]
    nc = N // CHUNK

    def chunks(t):
        return t.astype(jnp.float32).reshape(B, H, nc, CHUNK, t.shape[-1]).transpose(2, 0, 1, 3, 4)

    lower = jnp.tril(jnp.ones((CHUNK, CHUNK), dtype=bool))[:, :, None]

    def step(S, inp):
        qc, kc, vc, lf = inp
        b = jnp.cumsum(lf, axis=2)
        diff = b[:, :, :, None, :] - b[:, :, None, :, :]
        decay = jnp.exp(jnp.where(lower, diff, -jnp.inf))
        att = jnp.einsum('bhtsk,bhsk->bhts', qc[:, :, :, None, :] * decay, kc)
        o = jnp.einsum('bhts,bhsv->bhtv', att, vc) + jnp.einsum('bhtk,bhkv->bhtv', qc * jnp.exp(b), S)
        b_end = b[:, :, -1:, :]
        S = (jnp.exp(b_end[:, :, 0, :])[..., None] * S
             + jnp.einsum('bhsk,bhsv->bhkv', kc * jnp.exp(b_end - b), vc))
        return S, o

    S, o = lax.scan(step, s0, (chunks(q), chunks(k), chunks(v), chunks(logf)))
    return o.transpose(1, 2, 0, 3, 4).reshape(B, H, N, dv), S


def _hgrn_project(t, w_in, lb):
    Bn, N, _ = t.shape
    F = FORGET_DIM
    q, a_f, a_b, v, z = jnp.split(t @ w_in, [F, 2 * F, 3 * F, 3 * F + D_INNER], axis=-1)

    def hd(a, d):
        return a.reshape(Bn, N, HGRN_HEADS, d).transpose(0, 2, 1, 3)

    f_f = lb[0] + (1.0 - lb[0]) * jax.nn.sigmoid(a_f.astype(jnp.float32))
    f_b = lb[1] + (1.0 - lb[1]) * jax.nn.sigmoid(a_b.astype(jnp.float32))
    return (hd(jax.nn.silu(q), HGRN_KDIM), hd(1.0 - f_f, HGRN_KDIM), hd(jnp.log(f_f), HGRN_KDIM),
            hd(1.0 - f_b, HGRN_KDIM), hd(jnp.log(f_b), HGRN_KDIM), hd(v, HEAD_DIM), z)


def _hgrn_out(o, z, g, w_out):
    B, H, N, dv = o.shape
    o = o.transpose(0, 2, 1, 3)
    o = o * lax.rsqrt(jnp.mean(o * o, axis=-1, keepdims=True) + EPS)
    o = (o.reshape(B, N, H * dv) * g.astype(jnp.float32)).astype(z.dtype)
    return (o * jax.nn.silu(z)) @ w_out


def hgrn_mixer(h, hc, layer, w_in, lb_raw, g_norm, w_out, need_ctx):
    probs = jax.nn.softmax(lb_raw.astype(jnp.float32), axis=1)
    lb = (jnp.cumsum(probs, axis=1) - probs[:, :1])[:, layer]
    q, kf, lff, kb, lfb, v, z = _hgrn_project(h, w_in, lb)
    qc, kfc, lffc, kbc, lfbc, vc, zc = _hgrn_project(hc, w_in, lb)
    s0 = jnp.zeros((h.shape[0], HGRN_HEADS, HGRN_KDIM, HEAD_DIM), jnp.float32)

    def flip(t):
        return jnp.flip(t, axis=2)

    oc_f, s_f = hgrn_scan(qc, kfc, vc, lffc, s0)
    oc_b, s_b = hgrn_scan(flip(qc), flip(kbc), flip(vc), flip(lfbc), s0)
    o_f, _ = hgrn_scan(q, kf, v, lff, s_f)
    o_b, _ = hgrn_scan(flip(q), flip(kb), flip(v), flip(lfb), s_b)
    y = _hgrn_out(o_f + flip(o_b), z, g_norm, w_out)
    yc = _hgrn_out(oc_f + flip(oc_b), zc, g_norm, w_out) if need_ctx else None
    return y, yc


def setup_inputs(seed: int = 0) -> dict:
    key = jax.random.key(seed)
    ks = jax.random.split(key, 20)
    f32 = jnp.float32

    def nrm(k, shape, s):
        return jax.random.normal(k, shape, f32) * s

    return {
        'x': nrm(ks[0], (BATCH, SEQ, D_MODEL), 1.0),
        'c': nrm(ks[1], (BATCH, D_MODEL), 1.0),
        'ctx': nrm(ks[2], (BATCH, CTX_LEN, D_MODEL), 1.0),
        'c_ctx': nrm(ks[3], (D_MODEL,), 1.0),
        'w_mod': nrm(ks[4], (DEPTH, D_MODEL, 3 * D_MODEL), MOD_STD * D_MODEL ** -0.5),
        'b_mod': nrm(ks[5], (DEPTH, 3 * D_MODEL), 0.02),
        'g_pre': 1.0 + nrm(ks[6], (DEPTH, D_MODEL), 0.02),
        'g_post': 1.0 + nrm(ks[7], (DEPTH, D_MODEL), 0.02),
        'na_w_in': nrm(ks[8], (N_NA, D_MODEL, 4 * D_INNER), D_MODEL ** -0.5),
        'na_rpb': nrm(ks[9], (N_NA, N_HEADS, RPB_H, RPB_W), 0.1),
        'na_w_out': nrm(ks[10], (N_NA, D_INNER, D_MODEL), D_INNER ** -0.5),
        'pool_w_in': nrm(ks[11], (N_POOL, D_MODEL, 2 * D_INNER), D_MODEL ** -0.5),
        'pool_w_grp': nrm(ks[12], (N_POOL, N_POOL_GROUPS, POOL_GROUP, POOL_GROUP), POOL_GROUP ** -0.5),
        'pool_scale': 1.0 + nrm(ks[13], (N_POOL, D_INNER), 0.02),
        'pool_w_out': nrm(ks[14], (N_POOL, D_INNER, D_MODEL), D_INNER ** -0.5),
        'hgrn_w_in': nrm(ks[15], (N_HGRN, D_MODEL, 3 * FORGET_DIM + 2 * D_INNER), D_MODEL ** -0.5),
        'hgrn_lb': nrm(ks[16], (2, DEPTH, FORGET_DIM), 0.1),
        'hgrn_gnorm': 1.0 + nrm(ks[17], (N_HGRN, D_INNER), 0.02),
        'hgrn_w_out': nrm(ks[18], (N_HGRN, D_INNER, D_MODEL), D_INNER ** -0.5),
    }


def reference(x, c, ctx, c_ctx, w_mod, b_mod, g_pre, g_post, na_w_in, na_rpb, na_w_out,
              pool_w_in, pool_w_grp, pool_scale, pool_w_out, hgrn_w_in, hgrn_lb, hgrn_gnorm, hgrn_w_out):
    xc = ctx
    for i in range(DEPTH):
        kind, j = i % N_MIXERS, i // N_MIXERS
        need_ctx = i < DEPTH - 1
        mod = jax.nn.silu(c) @ w_mod[i] + b_mod[i]
        shift, scale, gate = jnp.split(mod[:, None, :], 3, axis=-1)
        modc = jax.nn.silu(c_ctx) @ w_mod[i] + b_mod[i]
        shift_c, scale_c, gate_c = jnp.split(modc, 3, axis=-1)
        h = rmsnorm(x, g_pre[i]) * (1.0 + scale) + shift
        hc = rmsnorm(xc, g_pre[i]) * (1.0 + scale_c) + shift_c
        if kind == 0:
            y, yc = na_mixer(h, hc, na_w_in[j], na_rpb[j], na_w_out[j], need_ctx)
        elif kind == 1:
            y, yc = pool_mixer(h, hc, pool_w_in[j], pool_w_grp[j], pool_scale[j], pool_w_out[j], need_ctx)
        else:
            y, yc = hgrn_mixer(h, hc, i, hgrn_w_in[j], hgrn_lb, hgrn_gnorm[j], hgrn_w_out[j], need_ctx)
        x = x + gate * rmsnorm(y, g_post[i])
        if need_ctx:
            xc = xc + gate_c * rmsnorm(yc, g_post[i])
    return x
```

```python
import functools

import numpy as np
import jax
import jax.numpy as jnp
from jax import lax
from jax.experimental import pallas as pl
from jax.experimental.pallas import tpu as pltpu

F32 = jnp.float32
BF16 = jnp.bfloat16

LANE = 128
VMEM_LIMIT = 56 * 1024 * 1024

EPS = 1e-6
GRID_W = 64
WIN_H = 8
WIN_W = 16
POOL_SIZES = (2, 4, 8, 16)
NEG = -1e30

NA_QROWS = 4
NA_BAND = NA_QROWS + WIN_H - 1
SUB = 16
SLAB = 128
HM_ROWS = 400


def _silu(x):
    return x / (1.0 + jnp.exp(-x))


def _nt_dot(a, b):
    return lax.dot_general(a, b, (((1,), (1,)), ((), ())), preferred_element_type=F32)


def _dot(a, b):
    return jnp.dot(a, b, preferred_element_type=F32)


def _params(sem, vmem=VMEM_LIMIT):
    return pltpu.CompilerParams(dimension_semantics=sem, vmem_limit_bytes=vmem)


def _mod_kernel(c_ref, w_ref, b_ref, o_ref):
    s = _silu(c_ref[...])
    s_hi = s.astype(BF16)
    s_lo = (s - s_hi.astype(F32)).astype(BF16)
    w = w_ref[0]
    w_hi = w.astype(BF16)
    w_lo = (w - w_hi.astype(F32)).astype(BF16)
    acc = _dot(s_hi, w_hi) + _dot(s_lo, w_hi) + _dot(s_hi, w_lo)
    o_ref[0] = acc + b_ref[0]


def _mod_call(c, c_ctx, w_mod, b_mod):
    depth, d, n3 = w_mod.shape
    nb = c.shape[0]
    assert nb + 1 <= 8
    cs = jnp.zeros((8, d), F32).at[:nb].set(c).at[nb].set(c_ctx)
    tn = 1024 if n3 % 1024 == 0 else n3
    return pl.pallas_call(
        _mod_kernel,
        grid=(depth, n3 // tn),
        in_specs=[
            pl.BlockSpec((8, d), lambda l, j: (0, 0)),
            pl.BlockSpec((1, d, tn), lambda l, j: (l, 0, j)),
            pl.BlockSpec((1, 1, tn), lambda l, j: (l, 0, j)),
        ],
        out_specs=pl.BlockSpec((1, 8, tn), lambda l, j: (l, 0, j)),
        out_shape=jax.ShapeDtypeStruct((depth, 8, n3), F32),
        compiler_params=_params(("parallel", "parallel")),
        name="mod_vectors",
    )(cs, w_mod, b_mod.reshape(depth, 1, n3))


def _proj_kernel(x_ref, g_ref, sc_ref, sh_ref, w_ref, o_ref, h_ref, *, ncol):
    @pl.when(pl.program_id(1) == 0)
    def _():
        x = x_ref[...]
        ms = jnp.mean(x * x, axis=-1, keepdims=True)
        y = x * lax.rsqrt(ms + EPS) * g_ref[...]
        h_ref[...] = (y * (1.0 + sc_ref[0]) + sh_ref[0]).astype(BF16)

    acc = _dot(h_ref[...], w_ref[...])
    for c in range(ncol):
        o_ref[c] = acc[:, c * LANE:(c + 1) * LANE].astype(BF16)


def _proj_call(x2d, g, scale, shift, w, rows_per_batch, name):
    r, d = x2d.shape
    n = w.shape[1]
    tm = min(512, rows_per_batch)
    tn = 1024 if n % 1024 == 0 else n
    assert r % tm == 0 and rows_per_batch % tm == 0 and n % tn == 0 and tn % LANE == 0
    ncol = tn // LANE
    bidx = lambda i, j: ((i * tm) // rows_per_batch, 0, 0)
    return pl.pallas_call(
        functools.partial(_proj_kernel, ncol=ncol),
        grid=(r // tm, n // tn),
        in_specs=[
            pl.BlockSpec((tm, d), lambda i, j: (i, 0)),
            pl.BlockSpec((1, d), lambda i, j: (0, 0)),
            pl.BlockSpec((1, 1, d), bidx),
            pl.BlockSpec((1, 1, d), bidx),
            pl.BlockSpec((d, tn), lambda i, j: (0, j)),
        ],
        out_specs=pl.BlockSpec((ncol, tm, LANE), lambda i, j: (j, i, 0)),
        out_shape=jax.ShapeDtypeStruct((n // LANE, r, LANE), BF16),
        scratch_shapes=[pltpu.VMEM((tm, d), BF16)],
        compiler_params=_params(("parallel", "arbitrary")),
        name=name,
    )(x2d, g.reshape(1, d), scale, shift, w)


def _out_kernel(u_ref, w_ref, x_ref, gate_ref, gp_ref, o_ref, acc_ref, *, nk):
    k = pl.program_id(1)

    @pl.when(k == 0)
    def _():
        acc_ref[...] = jnp.zeros_like(acc_ref)

    acc_ref[...] += _dot(u_ref[...], w_ref[...])

    @pl.when(k == nk - 1)
    def _():
        y = acc_ref[...]
        ms = jnp.mean(y * y, axis=-1, keepdims=True)
        r = y * lax.rsqrt(ms + EPS) * gp_ref[...]
        o_ref[...] = x_ref[...] + gate_ref[0] * r


def _out_call(u, w, x2d, gate, g_post, rows_per_batch, name):
    r, kdim = u.shape
    d = w.shape[1]
    tm = min(512, rows_per_batch)
    tk = 1024 if kdim % 1024 == 0 else kdim
    nk = kdim // tk
    assert r % tm == 0 and rows_per_batch % tm == 0
    return pl.pallas_call(
        functools.partial(_out_kernel, nk=nk),
        grid=(r // tm, nk),
        in_specs=[
            pl.BlockSpec((tm, tk), lambda i, k: (i, k)),
            pl.BlockSpec((tk, d), lambda i, k: (k, 0)),
            pl.BlockSpec((tm, d), lambda i, k: (i, 0)),
            pl.BlockSpec((1, 1, d), lambda i, k: ((i * tm) // rows_per_batch, 0, 0)),
            pl.BlockSpec((1, d), lambda i, k: (0, 0)),
        ],
        out_specs=pl.BlockSpec((tm, d), lambda i, k: (i, 0)),
        out_shape=jax.ShapeDtypeStruct((r, d), F32),
        scratch_shapes=[pltpu.VMEM((tm, d), F32)],
        compiler_params=_params(("parallel", "arbitrary")),
        name=name,
    )(u, w, x2d, gate, g_post.reshape(1, d))


def _na_bias(rpb, rows):
    col = np.arange(GRID_W)
    c0 = np.clip(col - WIN_W // 2, 0, GRID_W - WIN_W)
    col_in = (col[None, :] >= c0[:, None]) & (col[None, :] < c0[:, None] + WIN_W)
    dci = np.clip(col[None, :] - col[:, None], -(WIN_W - 1), WIN_W - 1) + (WIN_W - 1)
    nh = rpb.shape[0]
    last_q = rows - NA_QROWS
    outs = []
    for qs, bs in ((0, 0), (NA_QROWS, 0), (last_q, rows - NA_BAND)):
        rq = qs + np.arange(NA_QROWS)
        rk = bs + np.arange(NA_BAND)
        r0 = np.clip(rq - WIN_H // 2, 0, rows - WIN_H)
        row_in = (rk[None, :] >= r0[:, None]) & (rk[None, :] < r0[:, None] + WIN_H)
        dri = np.clip(rk[None, :] - rq[:, None] + (WIN_H - 1), 0, 2 * WIN_H - 2)
        g = rpb[:, dri[:, :, None, None], dci[None, None, :, :]]
        valid = row_in[:, :, None, None] & col_in[None, None, :, :]
        g = jnp.where(valid[None], g, NEG)
        outs.append(g.transpose(0, 1, 3, 2, 4).reshape(nh, NA_QROWS * GRID_W, NA_BAND * GRID_W))
    return jnp.stack(outs, axis=1).astype(F32)


def _na_kernel(q_ref, k_ref, v_ref, z_ref, kc_ref, vc_ref, bias_ref, o_ref, *, rows):
    qb = pl.program_id(2)
    nqb = rows // NA_QROWS
    var = jnp.where(qb == 0, 0, jnp.where(qb == nqb - 1, 2, 1))
    rs = jnp.clip(qb * NA_QROWS - WIN_H // 2, 0, rows - NA_BAND)
    start = pl.multiple_of(rs * GRID_W, GRID_W)
    nkeys = NA_BAND * GRID_W
    q = (q_ref[0].astype(F32) * (LANE ** -0.5)).astype(BF16)
    kb = k_ref[0, pl.ds(start, nkeys), :]
    vb = v_ref[0, pl.ds(start, nkeys), :]
    s1 = _nt_dot(q, kb) + bias_ref[0, var]
    s2 = _nt_dot(q, kc_ref[0])
    m = jnp.maximum(jnp.max(s1, axis=-1, keepdims=True), jnp.max(s2, axis=-1, keepdims=True))
    p1 = jnp.exp(s1 - m)
    p2 = jnp.exp(s2 - m)
    l = jnp.sum(p1, axis=-1, keepdims=True) + jnp.sum(p2, axis=-1, keepdims=True)
    o = (_dot(p1.astype(BF16), vb) + _dot(p2.astype(BF16), vc_ref[0])) / l
    o_ref[...] = (o * _silu(z_ref[0].astype(F32))).astype(BF16)


def _na_call(p, pc, bias, nb, t, m, nh):
    rows = t // GRID_W
    tq = NA_QROWS * GRID_W
    nqb = t // tq
    nq, nk = tq, NA_BAND * GRID_W
    return pl.pallas_call(
        functools.partial(_na_kernel, rows=rows),
        grid=(nh, nb, nqb),
        in_specs=[
            pl.BlockSpec((1, tq, LANE), lambda h, b, i: (h, b * nqb + i, 0)),
            pl.BlockSpec((1, t, LANE), lambda h, b, i: (nh + h, b, 0)),
            pl.BlockSpec((1, t, LANE), lambda h, b, i: (2 * nh + h, b, 0)),
            pl.BlockSpec((1, tq, LANE), lambda h, b, i: (3 * nh + h, b * nqb + i, 0)),
            pl.BlockSpec((1, m, LANE), lambda h, b, i: (nh + h, b, 0)),
            pl.BlockSpec((1, m, LANE), lambda h, b, i: (2 * nh + h, b, 0)),
            pl.BlockSpec((1, 3, nq, nk), lambda h, b, i: (h, 0, 0, 0)),
        ],
        out_specs=pl.BlockSpec((tq, LANE), lambda h, b, i: (b * nqb + i, h)),
        out_shape=jax.ShapeDtypeStruct((nb * t, nh * LANE), BF16),
        compiler_params=_params(("parallel", "parallel", "parallel")),
        name="na_attention",
    )(p, p, p, p, pc, pc, bias)


def _ctx_attn_kernel(q_ref, k_ref, v_ref, z_ref, o_ref):
    q = (q_ref[0].astype(F32) * (LANE ** -0.5)).astype(BF16)
    s = _nt_dot(q, k_ref[0])
    m = jnp.max(s, axis=-1, keepdims=True)
    p = jnp.exp(s - m)
    l = jnp.sum(p, axis=-1, keepdims=True)
    o = _dot(p.astype(BF16), v_ref[0]) / l
    o_ref[...] = (o * _silu(z_ref[0].astype(F32))).astype(BF16)


def _ctx_attn_call(pc, nb, m, nh):
    return pl.pallas_call(
        _ctx_attn_kernel,
        grid=(nh, nb),
        in_specs=[
            pl.BlockSpec((1, m, LANE), lambda h, b: (h, b, 0)),
            pl.BlockSpec((1, m, LANE), lambda h, b: (nh + h, b, 0)),
            pl.BlockSpec((1, m, LANE), lambda h, b: (2 * nh + h, b, 0)),
            pl.BlockSpec((1, m, LANE), lambda h, b: (3 * nh + h, b, 0)),
        ],
        out_specs=pl.BlockSpec((m, LANE), lambda h, b: (b, h)),
        out_shape=jax.ShapeDtypeStruct((nb * m, nh * LANE), BF16),
        compiler_params=_params(("parallel", "parallel")),
        name="ctx_attention",
    )(pc, pc, pc, pc)


def _pool_kernel(u_ref, o_ref, *, n, cols_per_group):
    group = pl.program_id(1) // cols_per_group
    t = lax.broadcasted_iota(jnp.int32, (n, LANE), 0)

    def up(a, k):
        return jnp.where(t < n - k, pltpu.roll(a, n - k, axis=0), 0.0)

    def down(a, k):
        return jnp.where(t >= k, pltpu.roll(a, k, axis=0), 0.0)

    for gi, w in enumerate(POOL_SIZES):
        half = w // 2

        @pl.when(group == gi)
        def _(half=half):
            x = u_ref[0].astype(F32)
            fwd, bwd, k = x, x, 1
            while k < half:
                fwd = fwd + up(fwd, k)
                bwd = bwd + down(bwd, k)
                k *= 2
            total = fwd + down(bwd, 1)
            cnt = (jnp.minimum(t + half, n) - jnp.maximum(t - half, 0)).astype(F32)
            o_ref[0] = (total / cnt - x).astype(BF16)


def _pool_call(p, nb, n, ncol):
    return pl.pallas_call(
        functools.partial(_pool_kernel, n=n, cols_per_group=ncol // len(POOL_SIZES)),
        grid=(nb, ncol),
        in_specs=[pl.BlockSpec((1, n, LANE), lambda b, c: (c, b, 0))],
        out_specs=pl.BlockSpec((1, n, LANE), lambda b, c: (c, b, 0)),
        out_shape=jax.ShapeDtypeStruct((ncol, nb * n, LANE), BF16),
        compiler_params=_params(("parallel", "parallel")),
        name="pool_centre",
    )(p)


def _grp_kernel(d_ref, w_ref, z_ref, s_ref, o_ref, *, kcol, ncol):
    lhs = jnp.concatenate([d_ref[c] for c in range(kcol)], axis=-1)
    acc = _dot(lhs, w_ref[0])
    z = jnp.concatenate([z_ref[c] for c in range(ncol)], axis=-1).astype(F32)
    o_ref[...] = (acc * s_ref[...] * _silu(z)).astype(BF16)


def _grp_call(dpool, p, w_grp, scale, rows_per_batch):
    ng, cg, _ = w_grp.shape
    ncol_all, r, _ = dpool.shape
    tm = min(512, rows_per_batch)
    tn = min(512, cg)
    kcol, ncol, nj = cg // LANE, tn // LANE, cg // tn
    zoff = ncol_all // ncol
    return pl.pallas_call(
        functools.partial(_grp_kernel, kcol=kcol, ncol=ncol),
        grid=(ng, r // tm, nj),
        in_specs=[
            pl.BlockSpec((kcol, tm, LANE), lambda g, i, j: (g, i, 0)),
            pl.BlockSpec((1, cg, tn), lambda g, i, j: (g, 0, j)),
            pl.BlockSpec((ncol, tm, LANE), lambda g, i, j: (zoff + g * nj + j, i, 0)),
            pl.BlockSpec((1, tn), lambda g, i, j: (0, g * nj + j)),
        ],
        out_specs=pl.BlockSpec((tm, tn), lambda g, i, j: (i, g * nj + j)),
        out_shape=jax.ShapeDtypeStruct((r, ng * cg), BF16),
        compiler_params=_params(("parallel", "parallel", "parallel")),
        name="pool_group_matmul",
    )(dpool, w_grp, p, scale.reshape(1, ng * cg))


def _hgrn_mats():
    t = np.arange(SLAB)
    blk, pos = t // SUB, t % SUB
    same = blk[:, None] == blk[None, :]
    out = []
    for d in (0, 1):
        if d == 0:
            cum = same & (pos[None, :] <= pos[:, None])
            mid = same & (pos[None, :] <= SUB // 2 - 1)
        else:
            cum = same & (pos[None, :] >= pos[:, None])
            mid = same & (pos[None, :] >= SUB // 2)
        mat = np.zeros((HM_ROWS, SLAB), np.float32)
        mat[0:SLAB] = cum
        mat[SLAB:2 * SLAB] = cum.astype(np.float32) - mid.astype(np.float32)
        mat[2 * SLAB:3 * SLAB] = same.astype(np.float32) - cum.astype(np.float32)
        mat[3 * SLAB:3 * SLAB + SLAB // SUB] = np.arange(SLAB // SUB)[:, None] == blk[None, :]
        out.append(mat)
    return jnp.asarray(np.stack(out), BF16)


def _hgrn_kernel(q_ref, af_ref, ab_ref, v_ref, z_ref, qc_ref, afc_ref, abc_ref, vc_ref, zc_ref,
                 lb_ref, gn_ref, hm_ref, u_ref, uc_ref,
                 o_scr, qh_scr, kb_scr, vt_scr, g_scr, s_scr, *, layer, t, m):
    nblk_c = m // SUB
    nblk = (m + t) // SUB
    per_slab = SLAB // SUB

    lbs = []
    for d in (0, 1):
        raw = lb_ref[d]
        e = jnp.exp(raw - jnp.max(raw, axis=0, keepdims=True))
        probs = e / jnp.sum(e, axis=0, keepdims=True)
        lbs.append(jnp.sum(probs[1:layer + 1], axis=0, keepdims=True) if layer >= 1
                   else jnp.zeros((1, LANE), F32))

    ti = lax.broadcasted_iota(jnp.int32, (SLAB, SLAB), 0)
    si = lax.broadcasted_iota(jnp.int32, (SLAB, SLAB), 1)
    same = (ti // SUB) == (si // SUB)
    masks = (same & (si <= ti), same & (si >= ti))

    def prep_slab(srcs, r0, slab):
        row = pl.multiple_of(slab * SLAB, SLAB)
        qs = _silu(srcs[0][0, pl.ds(r0, SLAB), :].astype(F32))
        v = srcs[3][0, pl.ds(r0, SLAB), :]
        vt_scr[slab] = v.astype(F32).T.astype(BF16)
        for d in (0, 1):
            a = srcs[1 + d][0, pl.ds(r0, SLAB), :].astype(F32)
            f = lbs[d] + (1.0 - lbs[d]) * (1.0 / (1.0 + jnp.exp(-a)))
            lf = jnp.log(f)
            kk = 1.0 - f
            hi = lf.astype(BF16)
            lo = (lf - hi.astype(F32)).astype(BF16)
            hm = hm_ref[d]
            r = _dot(hm, hi) + _dot(hm, lo)
            b_cum, b_mid, b_rest = r[0:SLAB], r[SLAB:2 * SLAB], r[2 * SLAB:3 * SLAB]
            gtot = jnp.exp(r[3 * SLAB:3 * SLAB + per_slab])
            qt = (qs * jnp.exp(b_mid)).astype(BF16)
            kt = (kk * jnp.exp(-b_mid)).astype(BF16)
            att = jnp.where(masks[d], _nt_dot(qt, kt), 0.0).astype(BF16)
            oi = _dot(att, v)
            if d == 0:
                o_scr[pl.ds(row, SLAB), :] = oi
            else:
                o_scr[pl.ds(row, SLAB), :] += oi
            qh_scr[d, pl.ds(row, SLAB), :] = (qs * jnp.exp(b_cum)).astype(BF16)
            kb_scr[d, pl.ds(row, SLAB), :] = (kk * jnp.exp(b_rest)).astype(BF16)
            g_scr[d, pl.ds(pl.multiple_of(slab * per_slab, per_slab), per_slab), :] = gtot

    csrc = (qc_ref, afc_ref, abc_ref, vc_ref)
    lsrc = (q_ref, af_ref, ab_ref, v_ref)
    for s in range(m // SLAB):
        prep_slab(csrc, s * SLAB, s)

    def prep_body(s, carry):
        prep_slab(lsrc, pl.multiple_of(s * SLAB, SLAB), s + m // SLAB)
        return carry

    lax.fori_loop(0, t // SLAB, prep_body, 0)

    s_scr[...] = jnp.zeros_like(s_scr)
    rowi = lax.broadcasted_iota(jnp.int32, (SLAB, LANE), 0) // SUB

    def step(i, carry):
        blk_f = i
        blk_b = jnp.where(i < nblk_c, nblk_c - 1 - i, nblk + nblk_c - 1 - i)
        for d, blk in ((0, blk_f), (1, blk_b)):
            slab = blk // per_slab
            sub = blk % per_slab
            row0 = pl.multiple_of(blk * SUB, SUB)
            st = s_scr[d]
            qh = qh_scr[d, pl.ds(row0, SUB), :]
            o_scr[pl.ds(row0, SUB), :] += _nt_dot(qh, st.astype(BF16))
            kb = kb_scr[d, pl.ds(pl.multiple_of(slab * SLAB, SLAB), SLAB), :]
            kb = jnp.where(rowi == sub, kb, jnp.zeros_like(kb))
            ds = _dot(vt_scr[slab], kb)
            s_scr[d] = g_scr[d, pl.ds(blk, 1), :] * st + ds
        return carry

    lax.fori_loop(0, nblk, step, 0)

    def finish(o, z_blk, out_ref):
        ms = jnp.mean(o * o, axis=-1, keepdims=True)
        on = o * lax.rsqrt(ms + EPS) * gn_ref[...]
        out_ref[...] = (on * _silu(z_blk.astype(F32))).astype(BF16)

    finish(o_scr[0:m, :], zc_ref[0], uc_ref)
    finish(o_scr[m:m + t, :], z_ref[0], u_ref)


def _hgrn_call(p, pc, lb_raw, g_norm, layer, nb, t, m, nh):
    depth = lb_raw.shape[1]
    tt = m + t
    lat = lambda k: pl.BlockSpec((1, t, LANE), lambda b, h, k=k: (k * nh + h, b, 0))
    ctx = lambda k: pl.BlockSpec((1, m, LANE), lambda b, h, k=k: (k * nh + h, b, 0))
    return pl.pallas_call(
        functools.partial(_hgrn_kernel, layer=layer, t=t, m=m),
        grid=(nb, nh),
        in_specs=[lat(k) for k in range(5)] + [ctx(k) for k in range(5)] + [
            pl.BlockSpec((2, depth, LANE), lambda b, h: (0, 0, h)),
            pl.BlockSpec((1, LANE), lambda b, h: (0, h)),
            pl.BlockSpec((2, HM_ROWS, SLAB), lambda b, h: (0, 0, 0)),
        ],
        out_specs=[
            pl.BlockSpec((t, LANE), lambda b, h: (b, h)),
            pl.BlockSpec((m, LANE), lambda b, h: (b, h)),
        ],
        out_shape=[
            jax.ShapeDtypeStruct((nb * t, nh * LANE), BF16),
            jax.ShapeDtypeStruct((nb * m, nh * LANE), BF16),
        ],
        scratch_shapes=[
            pltpu.VMEM((tt, LANE), F32),
            pltpu.VMEM((2, tt, LANE), BF16),
            pltpu.VMEM((2, tt, LANE), BF16),
            pltpu.VMEM((tt // SLAB, LANE, SLAB), BF16),
            pltpu.VMEM((2, tt // SUB, LANE), F32),
            pltpu.VMEM((2, LANE, LANE), F32),
        ],
        compiler_params=_params(("parallel", "parallel")),
        name="hgrn_scan",
    )(p, p, p, p, p, pc, pc, pc, pc, pc, lb_raw, g_norm.reshape(1, nh * LANE), _hgrn_mats())


def kernel(x, c, ctx, c_ctx, w_mod, b_mod, g_pre, g_post, na_w_in, na_rpb, na_w_out,
           pool_w_in, pool_w_grp, pool_scale, pool_w_out, hgrn_w_in, hgrn_lb, hgrn_gnorm, hgrn_w_out):
    nb, t, d = x.shape
    m = ctx.shape[1]
    depth = w_mod.shape[0]
    nh = na_w_out.shape[1] // LANE
    rows = t // GRID_W

    mods = _mod_call(c, c_ctx, w_mod, b_mod)
    xl = x.reshape(nb * t, d)
    xc = ctx.reshape(nb * m, d)

    for i in range(depth):
        kind, j = i % 3, i // 3
        need_ctx = i < depth - 1
        mod = mods[i]
        shift, scale, gate = (mod[:nb, None, k * d:(k + 1) * d] for k in range(3))
        shift_c, scale_c, gate_c = (mod[nb:nb + 1, None, k * d:(k + 1) * d] for k in range(3))
        if kind == 0:
            w_in, w_out = na_w_in[j], na_w_out[j]
        elif kind == 1:
            w_in, w_out = pool_w_in[j], pool_w_out[j]
        else:
            w_in, w_out = hgrn_w_in[j], hgrn_w_out[j]
        w_in = w_in.astype(BF16)
        w_out = w_out.astype(BF16)

        p = _proj_call(xl, g_pre[i], scale, shift, w_in, t, f"proj_lat_{i}")
        pc = _proj_call(xc, g_pre[i], scale_c, shift_c, w_in, nb * m, f"proj_ctx_{i}")

        uc = None
        if kind == 0:
            u = _na_call(p, pc, _na_bias(na_rpb[j], rows), nb, t, m, nh)
            if need_ctx:
                uc = _ctx_attn_call(pc, nb, m, nh)
        elif kind == 1:
            ncol = w_out.shape[0] // LANE
            wg = pool_w_grp[j].astype(BF16)
            u = _grp_call(_pool_call(p, nb, t, ncol), p, wg, pool_scale[j], t)
            if need_ctx:
                uc = _grp_call(_pool_call(pc, nb, m, ncol), pc, wg, pool_scale[j], m)
        else:
            u, uc = _hgrn_call(p, pc, hgrn_lb, hgrn_gnorm[j], i, nb, t, m, nh)

        xl = _out_call(u, w_out, xl, gate, g_post[i], t, f"out_lat_{i}")
        if need_ctx:
            xc = _out_call(uc, w_out, xc, gate_c, g_post[i], nb * m, f"out_ctx_{i}")

    return xl.reshape(nb, t, d)
```

```python
import functools

import numpy as np
import jax
import jax.numpy as jnp
from jax import lax
from jax.experimental import pallas as pl
from jax.experimental.pallas import tpu as pltpu

F32 = jnp.float32
BF16 = jnp.bfloat16

LANE = 128
VMEM_LIMIT = 56 * 1024 * 1024

EPS = 1e-6
GRID_W = 64
WIN_H = 8
WIN_W = 16
POOL_SIZES = (2, 4, 8, 16)
NEG = -1e30

NA_QROWS = 4
NA_BAND = NA_QROWS + WIN_H - 1
SUB = 16
SLAB = 128
HM_ROWS = 400


def _silu(x):
    return x / (1.0 + jnp.exp(-x))


def _nt_dot(a, b):
    return lax.dot_general(a, b, (((1,), (1,)), ((), ())), preferred_element_type=F32)


def _dot(a, b):
    return jnp.dot(a, b, preferred_element_type=F32)


def _params(sem, vmem=VMEM_LIMIT):
    return pltpu.CompilerParams(dimension_semantics=sem, vmem_limit_bytes=vmem)


def _mod_kernel(c_ref, w_ref, b_ref, o_ref):
    s = _silu(c_ref[...])
    s_hi = s.astype(BF16)
    s_lo = (s - s_hi.astype(F32)).astype(BF16)
    w = w_ref[0]
    w_hi = w.astype(BF16)
    w_lo = (w - w_hi.astype(F32)).astype(BF16)
    acc = _dot(s_hi, w_hi) + _dot(s_lo, w_hi) + _dot(s_hi, w_lo)
    o_ref[0] = acc + b_ref[0]


def _mod_call(c, c_ctx, w_mod, b_mod):
    depth, d, n3 = w_mod.shape
    nb = c.shape[0]
    assert nb + 1 <= 8
    cs = jnp.zeros((8, d), F32).at[:nb].set(c).at[nb].set(c_ctx)
    tn = 1024 if n3 % 1024 == 0 else n3
    return pl.pallas_call(
        _mod_kernel,
        grid=(depth, n3 // tn),
        in_specs=[
            pl.BlockSpec((8, d), lambda l, j: (0, 0)),
            pl.BlockSpec((1, d, tn), lambda l, j: (l, 0, j)),
            pl.BlockSpec((1, 1, tn), lambda l, j: (l, 0, j)),
        ],
        out_specs=pl.BlockSpec((1, 8, tn), lambda l, j: (l, 0, j)),
        out_shape=jax.ShapeDtypeStruct((depth, 8, n3), F32),
        compiler_params=_params(("parallel", "parallel")),
        name="mod_vectors",
    )(cs, w_mod, b_mod.reshape(depth, 1, n3))


def _proj_kernel(x_ref, g_ref, sc_ref, sh_ref, w_ref, o_ref, h_ref, *, ncol):
    @pl.when(pl.program_id(1) == 0)
    def _():
        x = x_ref[...]
        ms = jnp.mean(x * x, axis=-1, keepdims=True)
        y = x * lax.rsqrt(ms + EPS) * g_ref[...]
        h_ref[...] = (y * (1.0 + sc_ref[0]) + sh_ref[0]).astype(BF16)

    acc = _dot(h_ref[...], w_ref[...])
    for c in range(ncol):
        o_ref[c] = acc[:, c * LANE:(c + 1) * LANE].astype(BF16)


def _proj_call(x2d, g, scale, shift, w, rows_per_batch, name):
    r, d = x2d.shape
    n = w.shape[1]
    tm = min(512, rows_per_batch)
    tn = 1024 if n % 1024 == 0 else n
    assert r % tm == 0 and rows_per_batch % tm == 0 and n % tn == 0 and tn % LANE == 0
    ncol = tn // LANE
    bidx = lambda i, j: ((i * tm) // rows_per_batch, 0, 0)
    return pl.pallas_call(
        functools.partial(_proj_kernel, ncol=ncol),
        grid=(r // tm, n // tn),
        in_specs=[
            pl.BlockSpec((tm, d), lambda i, j: (i, 0)),
            pl.BlockSpec((1, d), lambda i, j: (0, 0)),
            pl.BlockSpec((1, 1, d), bidx),
            pl.BlockSpec((1, 1, d), bidx),
            pl.BlockSpec((d, tn), lambda i, j: (0, j)),
        ],
        out_specs=pl.BlockSpec((ncol, tm, LANE), lambda i, j: (j, i, 0)),
        out_shape=jax.ShapeDtypeStruct((n // LANE, r, LANE), BF16),
        scratch_shapes=[pltpu.VMEM((tm, d), BF16)],
        compiler_params=_params(("parallel", "arbitrary")),
        name=name,
    )(x2d, g.reshape(1, d), scale, shift, w)


def _out_kernel(u_ref, w_ref, x_ref, gate_ref, gp_ref, o_ref, acc_ref, *, nk):
    k = pl.program_id(1)

    @pl.when(k == 0)
    def _():
        acc_ref[...] = jnp.zeros_like(acc_ref)

    acc_ref[...] += _dot(u_ref[...], w_ref[...])

    @pl.when(k == nk - 1)
    def _():
        y = acc_ref[...]
        ms = jnp.mean(y * y, axis=-1, keepdims=True)
        r = y * lax.rsqrt(ms + EPS) * gp_ref[...]
        o_ref[...] = x_ref[...] + gate_ref[0] * r


def _out_call(u, w, x2d, gate, g_post, rows_per_batch, name):
    r, kdim = u.shape
    d = w.shape[1]
    tm = min(512, rows_per_batch)
    tk = 1024 if kdim % 1024 == 0 else kdim
    nk = kdim // tk
    assert r % tm == 0 and rows_per_batch % tm == 0
    return pl.pallas_call(
        functools.partial(_out_kernel, nk=nk),
        grid=(r // tm, nk),
        in_specs=[
            pl.BlockSpec((tm, tk), lambda i, k: (i, k)),
            pl.BlockSpec((tk, d), lambda i, k: (k, 0)),
            pl.BlockSpec((tm, d), lambda i, k: (i, 0)),
            pl.BlockSpec((1, 1, d), lambda i, k: ((i * tm) // rows_per_batch, 0, 0)),
            pl.BlockSpec((1, d), lambda i, k: (0, 0)),
        ],
        out_specs=pl.BlockSpec((tm, d), lambda i, k: (i, 0)),
        out_shape=jax.ShapeDtypeStruct((r, d), F32),
        scratch_shapes=[pltpu.VMEM((tm, d), F32)],
        compiler_params=_params(("parallel", "arbitrary")),
        name=name,
    )(u, w, x2d, gate, g_post.reshape(1, d))


def _na_block_plan(rows):
    plans = []
    for qs, bs in ((0, 0), (NA_QROWS, 0), (rows - NA_QROWS, rows - NA_BAND)):
        rq = qs + np.arange(NA_QROWS)
        rk = bs + np.arange(NA_BAND)
        r0 = np.clip(rq - WIN_H // 2, 0, rows - WIN_H)
        row_in = (rk[None, :] >= r0[:, None]) & (rk[None, :] < r0[:, None] + WIN_H)
        plans.append(np.where(row_in, rk[None, :] - rq[:, None] + (WIN_H - 1), -1))
    return plans


def _na_build_bias(rpb_ref, tb_scr, bias_scr, rows):
    nrel_w = 2 * WIN_W - 1
    lane = lax.broadcasted_iota(jnp.int32, (GRID_W, LANE), 1)
    cq = lax.broadcasted_iota(jnp.int32, (GRID_W, LANE), 0)
    ck = lane & (GRID_W - 1)
    didx = ck - cq + (WIN_W - 1)
    c0 = jnp.clip(cq - WIN_W // 2, 0, GRID_W - WIN_W)
    col_in = (ck >= c0) & (ck < c0 + WIN_W)
    for dr in range(2 * WIN_H - 1):
        tile = jnp.zeros((GRID_W, LANE), F32)
        for j in range(nrel_w):
            tile = jnp.where(didx == j, rpb_ref[0, 0, dr * nrel_w + j], tile)
        tb_scr[dr] = jnp.where(col_in, tile, NEG)
    neg = jnp.full((GRID_W, LANE), NEG, F32)
    for var, plan in enumerate(_na_block_plan(rows)):
        for rq in range(NA_QROWS):
            for pair in range((NA_BAND + 1) // 2):
                rks = [rk for rk in (2 * pair, 2 * pair + 1) if rk < NA_BAND]
                src = [tb_scr[int(plan[rq, rk])] if plan[rq, rk] >= 0 else neg for rk in rks]
                tile = src[0] if len(src) == 1 else jnp.where(lane < GRID_W, src[0], src[1])
                width = GRID_W * len(rks)
                bias_scr[var, rq * GRID_W:(rq + 1) * GRID_W, pair * LANE:pair * LANE + width] = tile[:, :width]


def _na_kernel(rpb_ref, q_ref, k_ref, v_ref, z_ref, kc_ref, vc_ref, o_ref, tb_scr, bias_scr, *, rows):
    qb = pl.program_id(2)
    nqb = rows // NA_QROWS

    @pl.when((pl.program_id(1) == 0) & (qb == 0))
    def _():
        _na_build_bias(rpb_ref, tb_scr, bias_scr, rows)

    var = jnp.where(qb == 0, 0, jnp.where(qb == nqb - 1, 2, 1))
    rs = jnp.clip(qb * NA_QROWS - WIN_H // 2, 0, rows - NA_BAND)
    start = pl.multiple_of(rs * GRID_W, GRID_W)
    nkeys = NA_BAND * GRID_W
    q = (q_ref[0].astype(F32) * (LANE ** -0.5)).astype(BF16)
    kb = k_ref[0, pl.ds(start, nkeys), :]
    vb = v_ref[0, pl.ds(start, nkeys), :]
    s1 = _nt_dot(q, kb) + bias_scr[var]
    s2 = _nt_dot(q, kc_ref[0])
    m = jnp.maximum(jnp.max(s1, axis=-1, keepdims=True), jnp.max(s2, axis=-1, keepdims=True))
    p1 = jnp.exp(s1 - m)
    p2 = jnp.exp(s2 - m)
    l = jnp.sum(p1, axis=-1, keepdims=True) + jnp.sum(p2, axis=-1, keepdims=True)
    o = (_dot(p1.astype(BF16), vb) + _dot(p2.astype(BF16), vc_ref[0])) / l
    o_ref[...] = (o * _silu(z_ref[0].astype(F32))).astype(BF16)


def _na_call(p, pc, rpb, nb, t, m, nh):
    rows = t // GRID_W
    tq = NA_QROWS * GRID_W
    nqb = t // tq
    nq, nk = tq, NA_BAND * GRID_W
    nrel = rpb.shape[1] * rpb.shape[2]
    return pl.pallas_call(
        functools.partial(_na_kernel, rows=rows),
        grid=(nh, nb, nqb),
        in_specs=[
            pl.BlockSpec((1, 1, nrel), lambda h, b, i: (h, 0, 0), memory_space=pltpu.SMEM),
            pl.BlockSpec((1, tq, LANE), lambda h, b, i: (h, b * nqb + i, 0)),
            pl.BlockSpec((1, t, LANE), lambda h, b, i: (nh + h, b, 0)),
            pl.BlockSpec((1, t, LANE), lambda h, b, i: (2 * nh + h, b, 0)),
            pl.BlockSpec((1, tq, LANE), lambda h, b, i: (3 * nh + h, b * nqb + i, 0)),
            pl.BlockSpec((1, m, LANE), lambda h, b, i: (nh + h, b, 0)),
            pl.BlockSpec((1, m, LANE), lambda h, b, i: (2 * nh + h, b, 0)),
        ],
        out_specs=pl.BlockSpec((tq, LANE), lambda h, b, i: (b * nqb + i, h)),
        out_shape=jax.ShapeDtypeStruct((nb * t, nh * LANE), BF16),
        scratch_shapes=[
            pltpu.VMEM((2 * WIN_H - 1, GRID_W, LANE), F32),
            pltpu.VMEM((3, nq, nk), F32),
        ],
        compiler_params=_params(("parallel", "arbitrary", "arbitrary")),
        name="na_attention",
    )(rpb.reshape(nh, 1, nrel), p, p, p, p, pc, pc)


def _ctx_attn_kernel(q_ref, k_ref, v_ref, z_ref, o_ref):
    q = (q_ref[0].astype(F32) * (LANE ** -0.5)).astype(BF16)
    s = _nt_dot(q, k_ref[0])
    m = jnp.max(s, axis=-1, keepdims=True)
    p = jnp.exp(s - m)
    l = jnp.sum(p, axis=-1, keepdims=True)
    o = _dot(p.astype(BF16), v_ref[0]) / l
    o_ref[...] = (o * _silu(z_ref[0].astype(F32))).astype(BF16)


def _ctx_attn_call(pc, nb, m, nh):
    return pl.pallas_call(
        _ctx_attn_kernel,
        grid=(nh, nb),
        in_specs=[
            pl.BlockSpec((1, m, LANE), lambda h, b: (h, b, 0)),
            pl.BlockSpec((1, m, LANE), lambda h, b: (nh + h, b, 0)),
            pl.BlockSpec((1, m, LANE), lambda h, b: (2 * nh + h, b, 0)),
            pl.BlockSpec((1, m, LANE), lambda h, b: (3 * nh + h, b, 0)),
        ],
        out_specs=pl.BlockSpec((m, LANE), lambda h, b: (b, h)),
        out_shape=jax.ShapeDtypeStruct((nb * m, nh * LANE), BF16),
        compiler_params=_params(("parallel", "parallel")),
        name="ctx_attention",
    )(pc, pc, pc, pc)


def _pool_kernel(u_ref, o_ref, *, n, cols_per_group):
    group = pl.program_id(1) // cols_per_group
    t = lax.broadcasted_iota(jnp.int32, (n, LANE), 0)

    def up(a, k):
        return jnp.where(t < n - k, pltpu.roll(a, n - k, axis=0), 0.0)

    def down(a, k):
        return jnp.where(t >= k, pltpu.roll(a, k, axis=0), 0.0)

    for gi, w in enumerate(POOL_SIZES):
        half = w // 2

        @pl.when(group == gi)
        def _(half=half):
            x = u_ref[0].astype(F32)
            fwd, bwd, k = x, x, 1
            while k < half:
                fwd = fwd + up(fwd, k)
                bwd = bwd + down(bwd, k)
                k *= 2
            total = fwd + down(bwd, 1)
            cnt = (jnp.minimum(t + half, n) - jnp.maximum(t - half, 0)).astype(F32)
            o_ref[0] = (total / cnt - x).astype(BF16)


def _pool_call(p, nb, n, ncol):
    return pl.pallas_call(
        functools.partial(_pool_kernel, n=n, cols_per_group=ncol // len(POOL_SIZES)),
        grid=(nb, ncol),
        in_specs=[pl.BlockSpec((1, n, LANE), lambda b, c: (c, b, 0))],
        out_specs=pl.BlockSpec((1, n, LANE), lambda b, c: (c, b, 0)),
        out_shape=jax.ShapeDtypeStruct((ncol, nb * n, LANE), BF16),
        compiler_params=_params(("parallel", "parallel")),
        name="pool_centre",
    )(p)


def _grp_kernel(d_ref, w_ref, z_ref, s_ref, o_ref, *, kcol, ncol):
    lhs = jnp.concatenate([d_ref[c] for c in range(kcol)], axis=-1)
    acc = _dot(lhs, w_ref[0])
    z = jnp.concatenate([z_ref[c] for c in range(ncol)], axis=-1).astype(F32)
    o_ref[...] = (acc * s_ref[...] * _silu(z)).astype(BF16)


def _grp_call(dpool, p, w_grp, scale, rows_per_batch):
    ng, cg, _ = w_grp.shape
    ncol_all, r, _ = dpool.shape
    tm = min(512, rows_per_batch)
    tn = min(512, cg)
    kcol, ncol, nj = cg // LANE, tn // LANE, cg // tn
    zoff = ncol_all // ncol
    return pl.pallas_call(
        functools.partial(_grp_kernel, kcol=kcol, ncol=ncol),
        grid=(ng, r // tm, nj),
        in_specs=[
            pl.BlockSpec((kcol, tm, LANE), lambda g, i, j: (g, i, 0)),
            pl.BlockSpec((1, cg, tn), lambda g, i, j: (g, 0, j)),
            pl.BlockSpec((ncol, tm, LANE), lambda g, i, j: (zoff + g * nj + j, i, 0)),
            pl.BlockSpec((1, tn), lambda g, i, j: (0, g * nj + j)),
        ],
        out_specs=pl.BlockSpec((tm, tn), lambda g, i, j: (i, g * nj + j)),
        out_shape=jax.ShapeDtypeStruct((r, ng * cg), BF16),
        compiler_params=_params(("parallel", "parallel", "parallel")),
        name="pool_group_matmul",
    )(dpool, w_grp, p, scale.reshape(1, ng * cg))


def _hgrn_mats():
    t = np.arange(SLAB)
    blk, pos = t // SUB, t % SUB
    same = blk[:, None] == blk[None, :]
    out = []
    for d in (0, 1):
        if d == 0:
            cum = same & (pos[None, :] <= pos[:, None])
            mid = same & (pos[None, :] <= SUB // 2 - 1)
        else:
            cum = same & (pos[None, :] >= pos[:, None])
            mid = same & (pos[None, :] >= SUB // 2)
        mat = np.zeros((HM_ROWS, SLAB), np.float32)
        mat[0:SLAB] = cum
        mat[SLAB:2 * SLAB] = cum.astype(np.float32) - mid.astype(np.float32)
        mat[2 * SLAB:3 * SLAB] = same.astype(np.float32) - cum.astype(np.float32)
        mat[3 * SLAB:3 * SLAB + SLAB // SUB] = np.arange(SLAB // SUB)[:, None] == blk[None, :]
        out.append(mat)
    return jnp.asarray(np.stack(out), BF16)


def _hgrn_kernel(q_ref, af_ref, ab_ref, v_ref, z_ref, qc_ref, afc_ref, abc_ref, vc_ref, zc_ref,
                 lb_ref, gn_ref, hm_ref, u_ref, uc_ref,
                 o_scr, qh_scr, kb_scr, vt_scr, g_scr, s_scr, *, layer, t, m):
    nblk_c = m // SUB
    nblk = (m + t) // SUB
    per_slab = SLAB // SUB

    lbs = []
    for d in (0, 1):
        raw = lb_ref[d]
        e = jnp.exp(raw - jnp.max(raw, axis=0, keepdims=True))
        probs = e / jnp.sum(e, axis=0, keepdims=True)
        lbs.append(jnp.sum(probs[1:layer + 1], axis=0, keepdims=True) if layer >= 1
                   else jnp.zeros((1, LANE), F32))

    ti = lax.broadcasted_iota(jnp.int32, (SLAB, SLAB), 0)
    si = lax.broadcasted_iota(jnp.int32, (SLAB, SLAB), 1)
    same = (ti // SUB) == (si // SUB)
    masks = (same & (si <= ti), same & (si >= ti))

    def prep_slab(srcs, r0, slab):
        row = pl.multiple_of(slab * SLAB, SLAB)
        qs = _silu(srcs[0][0, pl.ds(r0, SLAB), :].astype(F32))
        v = srcs[3][0, pl.ds(r0, SLAB), :]
        vt_scr[slab] = v.astype(F32).T.astype(BF16)
        for d in (0, 1):
            a = srcs[1 + d][0, pl.ds(r0, SLAB), :].astype(F32)
            f = lbs[d] + (1.0 - lbs[d]) * (1.0 / (1.0 + jnp.exp(-a)))
            lf = jnp.log(f)
            kk = 1.0 - f
            hi = lf.astype(BF16)
            lo = (lf - hi.astype(F32)).astype(BF16)
            hm = hm_ref[d]
            r = _dot(hm, hi) + _dot(hm, lo)
            b_cum, b_mid, b_rest = r[0:SLAB], r[SLAB:2 * SLAB], r[2 * SLAB:3 * SLAB]
            gtot = jnp.exp(r[3 * SLAB:3 * SLAB + per_slab])
            qt = (qs * jnp.exp(b_mid)).astype(BF16)
            kt = (kk * jnp.exp(-b_mid)).astype(BF16)
            att = jnp.where(masks[d], _nt_dot(qt, kt), 0.0).astype(BF16)
            oi = _dot(att, v)
            if d == 0:
                o_scr[pl.ds(row, SLAB), :] = oi
            else:
                o_scr[pl.ds(row, SLAB), :] += oi
            qh_scr[d, pl.ds(row, SLAB), :] = (qs * jnp.exp(b_cum)).astype(BF16)
            kb_scr[d, pl.ds(row, SLAB), :] = (kk * jnp.exp(b_rest)).astype(BF16)
            g_scr[d, pl.ds(pl.multiple_of(slab * per_slab, per_slab), per_slab), :] = gtot

    csrc = (qc_ref, afc_ref, abc_ref, vc_ref)
    lsrc = (q_ref, af_ref, ab_ref, v_ref)
    for s in range(m // SLAB):
        prep_slab(csrc, s * SLAB, s)

    def prep_body(s, carry):
        prep_slab(lsrc, pl.multiple_of(s * SLAB, SLAB), s + m // SLAB)
        return carry

    lax.fori_loop(0, t // SLAB, prep_body, 0, unroll=2)

    s_scr[...] = jnp.zeros_like(s_scr)
    rowi = lax.broadcasted_iota(jnp.int32, (SLAB, LANE), 0) // SUB

    def step(i, carry):
        blk_f = i
        blk_b = jnp.where(i < nblk_c, nblk_c - 1 - i, nblk + nblk_c - 1 - i)
        for d, blk in ((0, blk_f), (1, blk_b)):
            slab = blk // per_slab
            sub = blk % per_slab
            row0 = pl.multiple_of(blk * SUB, SUB)
            st = s_scr[d]
            qh = qh_scr[d, pl.ds(row0, SUB), :]
            o_scr[pl.ds(row0, SUB), :] += _nt_dot(qh, st.astype(BF16))
            kb = kb_scr[d, pl.ds(pl.multiple_of(slab * SLAB, SLAB), SLAB), :]
            kb = jnp.where(rowi == sub, kb, jnp.zeros_like(kb))
            ds = _dot(vt_scr[slab], kb)
            s_scr[d] = g_scr[d, pl.ds(blk, 1), :] * st + ds
        return carry

    lax.fori_loop(0, nblk, step, 0, unroll=4)

    def finish(o, z_blk, out_ref):
        ms = jnp.mean(o * o, axis=-1, keepdims=True)
        on = o * lax.rsqrt(ms + EPS) * gn_ref[...]
        out_ref[...] = (on * _silu(z_blk.astype(F32))).astype(BF16)

    finish(o_scr[0:m, :], zc_ref[0], uc_ref)
    finish(o_scr[m:m + t, :], z_ref[0], u_ref)


def _hgrn_call(p, pc, lb_raw, g_norm, layer, nb, t, m, nh):
    depth = lb_raw.shape[1]
    tt = m + t
    lat = lambda k: pl.BlockSpec((1, t, LANE), lambda b, h, k=k: (k * nh + h, b, 0))
    ctx = lambda k: pl.BlockSpec((1, m, LANE), lambda b, h, k=k: (k * nh + h, b, 0))
    return pl.pallas_call(
        functools.partial(_hgrn_kernel, layer=layer, t=t, m=m),
        grid=(nb, nh),
        in_specs=[lat(k) for k in range(5)] + [ctx(k) for k in range(5)] + [
            pl.BlockSpec((2, depth, LANE), lambda b, h: (0, 0, h)),
            pl.BlockSpec((1, LANE), lambda b, h: (0, h)),
            pl.BlockSpec((2, HM_ROWS, SLAB), lambda b, h: (0, 0, 0)),
        ],
        out_specs=[
            pl.BlockSpec((t, LANE), lambda b, h: (b, h)),
            pl.BlockSpec((m, LANE), lambda b, h: (b, h)),
        ],
        out_shape=[
            jax.ShapeDtypeStruct((nb * t, nh * LANE), BF16),
            jax.ShapeDtypeStruct((nb * m, nh * LANE), BF16),
        ],
        scratch_shapes=[
            pltpu.VMEM((tt, LANE), F32),
            pltpu.VMEM((2, tt, LANE), BF16),
            pltpu.VMEM((2, tt, LANE), BF16),
            pltpu.VMEM((tt // SLAB, LANE, SLAB), BF16),
            pltpu.VMEM((2, tt // SUB, LANE), F32),
            pltpu.VMEM((2, LANE, LANE), F32),
        ],
        compiler_params=_params(("parallel", "parallel")),
        name="hgrn_scan",
    )(p, p, p, p, p, pc, pc, pc, pc, pc, lb_raw, g_norm.reshape(1, nh * LANE), _hgrn_mats())


def kernel(x, c, ctx, c_ctx, w_mod, b_mod, g_pre, g_post, na_w_in, na_rpb, na_w_out,
           pool_w_in, pool_w_grp, pool_scale, pool_w_out, hgrn_w_in, hgrn_lb, hgrn_gnorm, hgrn_w_out):
    nb, t, d = x.shape
    m = ctx.shape[1]
    depth = w_mod.shape[0]
    nh = na_w_out.shape[1] // LANE
    rows = t // GRID_W

    mods = _mod_call(c, c_ctx, w_mod, b_mod)
    xl = x.reshape(nb * t, d)
    xc = ctx.reshape(nb * m, d)

    for i in range(depth):
        kind, j = i % 3, i // 3
        need_ctx = i < depth - 1
        mod = mods[i]
        shift, scale, gate = (mod[:nb, None, k * d:(k + 1) * d] for k in range(3))
        shift_c, scale_c, gate_c = (mod[nb:nb + 1, None, k * d:(k + 1) * d] for k in range(3))
        if kind == 0:
            w_in, w_out = na_w_in[j], na_w_out[j]
        elif kind == 1:
            w_in, w_out = pool_w_in[j], pool_w_out[j]
        else:
            w_in, w_out = hgrn_w_in[j], hgrn_w_out[j]
        w_in = w_in.astype(BF16)
        w_out = w_out.astype(BF16)

        p = _proj_call(xl, g_pre[i], scale, shift, w_in, t, f"proj_lat_{i}")
        pc = _proj_call(xc, g_pre[i], scale_c, shift_c, w_in, nb * m, f"proj_ctx_{i}")

        uc = None
        if kind == 0:
            u = _na_call(p, pc, na_rpb[j], nb, t, m, nh)
            if need_ctx:
                uc = _ctx_attn_call(pc, nb, m, nh)
        elif kind == 1:
            ncol = w_out.shape[0] // LANE
            wg = pool_w_grp[j].astype(BF16)
            u = _grp_call(_pool_call(p, nb, t, ncol), p, wg, pool_scale[j], t)
            if need_ctx:
                uc = _grp_call(_pool_call(pc, nb, m, ncol), pc, wg, pool_scale[j], m)
        else:
            u, uc = _hgrn_call(p, pc, hgrn_lb, hgrn_gnorm[j], i, nb, t, m, nh)

        xl = _out_call(u, w_out, xl, gate, g_post[i], t, f"out_lat_{i}")
        if need_ctx:
            xc = _out_call(uc, w_out, xc, gate_c, g_post[i], nb * m, f"out_ctx_{i}")

    return xl.reshape(nb, t, d)
```

```python
import functools

import numpy as np
import jax
import jax.numpy as jnp
from jax import lax
from jax.experimental import pallas as pl
from jax.experimental.pallas import tpu as pltpu

F32 = jnp.float32
BF16 = jnp.bfloat16

LANE = 128
VMEM_LIMIT = 56 * 1024 * 1024

EPS = 1e-6
GRID_W = 64
WIN_H = 8
WIN_W = 16
POOL_SIZES = (2, 4, 8, 16)
NEG = -1e30

NA_QROWS = 4
NA_HEADS = 2
LOG2E = 1.4426950408889634
NA_BAND = NA_QROWS + WIN_H - 1
SLAB = 128


def _silu(x):
    return x / (1.0 + jnp.exp(-x))


def _nt_dot(a, b):
    return lax.dot_general(a, b, (((1,), (1,)), ((), ())), preferred_element_type=F32)


def _dot(a, b):
    return jnp.dot(a, b, preferred_element_type=F32)


def _params(sem, vmem=VMEM_LIMIT):
    return pltpu.CompilerParams(dimension_semantics=sem, vmem_limit_bytes=vmem)


def _mod_kernel(c_ref, w_ref, b_ref, o_ref):
    s = _silu(c_ref[...])
    s_hi = s.astype(BF16)
    s_lo = (s - s_hi.astype(F32)).astype(BF16)
    w = w_ref[0]
    w_hi = w.astype(BF16)
    w_lo = (w - w_hi.astype(F32)).astype(BF16)
    acc = _dot(s_hi, w_hi) + _dot(s_lo, w_hi) + _dot(s_hi, w_lo)
    o_ref[0] = acc + b_ref[0]


def _mod_call(c, c_ctx, w_mod, b_mod):
    depth, d, n3 = w_mod.shape
    nb = c.shape[0]
    assert nb + 1 <= 8
    cs = jnp.zeros((8, d), F32).at[:nb].set(c).at[nb].set(c_ctx)
    tn = 1024 if n3 % 1024 == 0 else n3
    return pl.pallas_call(
        _mod_kernel,
        grid=(depth, n3 // tn),
        in_specs=[
            pl.BlockSpec((8, d), lambda l, j: (0, 0)),
            pl.BlockSpec((1, d, tn), lambda l, j: (l, 0, j)),
            pl.BlockSpec((1, 1, tn), lambda l, j: (l, 0, j)),
        ],
        out_specs=pl.BlockSpec((1, 8, tn), lambda l, j: (l, 0, j)),
        out_shape=jax.ShapeDtypeStruct((depth, 8, n3), F32),
        compiler_params=_params(("parallel", "parallel")),
        name="mod_vectors",
    )(cs, w_mod, b_mod.reshape(depth, 1, n3))


def _proj_kernel(x_ref, g_ref, sc_ref, sh_ref, w_ref, o_ref, h_ref, *, ncol):
    @pl.when(pl.program_id(1) == 0)
    def _():
        x = x_ref[...]
        ms = jnp.mean(x * x, axis=-1, keepdims=True)
        y = x * lax.rsqrt(ms + EPS) * g_ref[...]
        h_ref[...] = (y * (1.0 + sc_ref[0]) + sh_ref[0]).astype(BF16)

    acc = _dot(h_ref[...], w_ref[...])
    for c in range(ncol):
        o_ref[c] = acc[:, c * LANE:(c + 1) * LANE].astype(BF16)


def _proj_call(x2d, g, scale, shift, w, rows_per_batch, name):
    r, d = x2d.shape
    n = w.shape[1]
    tm = min(512, rows_per_batch)
    tn = 1024 if n % 1024 == 0 else n
    assert r % tm == 0 and rows_per_batch % tm == 0 and n % tn == 0 and tn % LANE == 0
    ncol = tn // LANE
    bidx = lambda i, j: ((i * tm) // rows_per_batch, 0, 0)
    return pl.pallas_call(
        functools.partial(_proj_kernel, ncol=ncol),
        grid=(r // tm, n // tn),
        in_specs=[
            pl.BlockSpec((tm, d), lambda i, j: (i, 0)),
            pl.BlockSpec((1, d), lambda i, j: (0, 0)),
            pl.BlockSpec((1, 1, d), bidx),
            pl.BlockSpec((1, 1, d), bidx),
            pl.BlockSpec((d, tn), lambda i, j: (0, j)),
        ],
        out_specs=pl.BlockSpec((ncol, tm, LANE), lambda i, j: (j, i, 0)),
        out_shape=jax.ShapeDtypeStruct((n // LANE, r, LANE), BF16),
        scratch_shapes=[pltpu.VMEM((tm, d), BF16)],
        compiler_params=_params(("parallel", "arbitrary")),
        name=name,
    )(x2d, g.reshape(1, d), scale, shift, w)


def _out_kernel(u_ref, w_ref, x_ref, gate_ref, gp_ref, o_ref, acc_ref, *, nk):
    k = pl.program_id(1)

    @pl.when(k == 0)
    def _():
        acc_ref[...] = jnp.zeros_like(acc_ref)

    acc_ref[...] += _dot(u_ref[...], w_ref[...])

    @pl.when(k == nk - 1)
    def _():
        y = acc_ref[...]
        ms = jnp.mean(y * y, axis=-1, keepdims=True)
        r = y * lax.rsqrt(ms + EPS) * gp_ref[...]
        o_ref[...] = x_ref[...] + gate_ref[0] * r


def _out_call(u, w, x2d, gate, g_post, rows_per_batch, name):
    r, kdim = u.shape
    d = w.shape[1]
    tm = min(512, rows_per_batch)
    tk = 1024 if kdim % 1024 == 0 else kdim
    nk = kdim // tk
    assert r % tm == 0 and rows_per_batch % tm == 0
    return pl.pallas_call(
        functools.partial(_out_kernel, nk=nk),
        grid=(r // tm, nk),
        in_specs=[
            pl.BlockSpec((tm, tk), lambda i, k: (i, k)),
            pl.BlockSpec((tk, d), lambda i, k: (k, 0)),
            pl.BlockSpec((tm, d), lambda i, k: (i, 0)),
            pl.BlockSpec((1, 1, d), lambda i, k: ((i * tm) // rows_per_batch, 0, 0)),
            pl.BlockSpec((1, d), lambda i, k: (0, 0)),
        ],
        out_specs=pl.BlockSpec((tm, d), lambda i, k: (i, 0)),
        out_shape=jax.ShapeDtypeStruct((r, d), F32),
        scratch_shapes=[pltpu.VMEM((tm, d), F32)],
        compiler_params=_params(("parallel", "arbitrary")),
        name=name,
    )(u, w, x2d, gate, g_post.reshape(1, d))


def _na_block_plan(rows):
    plans = []
    for qs, bs in ((0, 0), (NA_QROWS, 0), (rows - NA_QROWS, rows - NA_BAND)):
        rq = qs + np.arange(NA_QROWS)
        rk = bs + np.arange(NA_BAND)
        r0 = np.clip(rq - WIN_H // 2, 0, rows - WIN_H)
        row_in = (rk[None, :] >= r0[:, None]) & (rk[None, :] < r0[:, None] + WIN_H)
        plans.append(np.where(row_in, rk[None, :] - rq[:, None] + (WIN_H - 1), -1))
    return plans


def _na_build_bias(rpb_ref, hh, tb_scr, bias_scr, rows):
    nrel_w = 2 * WIN_W - 1
    lane = lax.broadcasted_iota(jnp.int32, (GRID_W, LANE), 1)
    cq = lax.broadcasted_iota(jnp.int32, (GRID_W, LANE), 0)
    ck = lane & (GRID_W - 1)
    didx = ck - cq + (WIN_W - 1)
    c0 = jnp.clip(cq - WIN_W // 2, 0, GRID_W - WIN_W)
    col_in = (ck >= c0) & (ck < c0 + WIN_W)
    for dr in range(2 * WIN_H - 1):
        tile = jnp.zeros((GRID_W, LANE), F32)
        for j in range(nrel_w):
            tile = jnp.where(didx == j, rpb_ref[0, hh, dr * nrel_w + j] * LOG2E, tile)
        tb_scr[hh, dr] = jnp.where(col_in, tile, NEG)
    neg = jnp.full((GRID_W, LANE), NEG, F32)
    for var, plan in enumerate(_na_block_plan(rows)):
        for rq in range(NA_QROWS):
            for pair in range((NA_BAND + 1) // 2):
                rks = [rk for rk in (2 * pair, 2 * pair + 1) if rk < NA_BAND]
                src = [tb_scr[hh, int(plan[rq, rk])] if plan[rq, rk] >= 0 else neg for rk in rks]
                tile = src[0] if len(src) == 1 else jnp.where(lane < GRID_W, src[0], src[1])
                width = GRID_W * len(rks)
                bias_scr[hh, var, rq * GRID_W:(rq + 1) * GRID_W,
                         pair * LANE:pair * LANE + width] = tile[:, :width]


def _na_kernel(rpb_ref, q_ref, k_ref, v_ref, z_ref, kc_ref, vc_ref, o_ref, tb_scr, bias_scr,
               *, rows, heads):
    qb = pl.program_id(2)
    nqb = rows // NA_QROWS

    @pl.when((pl.program_id(1) == 0) & (qb == 0))
    def _():
        for hh in range(heads):
            _na_build_bias(rpb_ref, hh, tb_scr, bias_scr, rows)

    var = jnp.where(qb == 0, 0, jnp.where(qb == nqb - 1, 2, 1))
    rs = jnp.clip(qb * NA_QROWS - WIN_H // 2, 0, rows - NA_BAND)
    start = pl.multiple_of(rs * GRID_W, GRID_W)
    nkeys = NA_BAND * GRID_W
    for hh in range(heads):
        q = (q_ref[hh].astype(F32) * (LANE ** -0.5 * LOG2E)).astype(BF16)
        kb = k_ref[hh, pl.ds(start, nkeys), :]
        vb = v_ref[hh, pl.ds(start, nkeys), :]
        s1 = _nt_dot(q, kb) + bias_scr[hh, var]
        s2 = _nt_dot(q, kc_ref[hh])
        mx = jnp.maximum(jnp.max(s1, axis=-1, keepdims=True), jnp.max(s2, axis=-1, keepdims=True))
        p1 = jnp.exp2(s1 - mx)
        p2 = jnp.exp2(s2 - mx)
        l = jnp.sum(p1, axis=-1, keepdims=True) + jnp.sum(p2, axis=-1, keepdims=True)
        o = (_dot(p1.astype(BF16), vb) + _dot(p2.astype(BF16), vc_ref[hh])) / l
        o_ref[:, hh * LANE:(hh + 1) * LANE] = (o * _silu(z_ref[hh].astype(F32))).astype(BF16)


def _na_call(p, pc, rpb, nb, t, m, nh):
    rows = t // GRID_W
    tq = NA_QROWS * GRID_W
    nqb = t // tq
    nq, nk = tq, NA_BAND * GRID_W
    nrel = rpb.shape[1] * rpb.shape[2]
    hb = NA_HEADS if nh % NA_HEADS == 0 else 1
    ng = nh // hb
    return pl.pallas_call(
        functools.partial(_na_kernel, rows=rows, heads=hb),
        grid=(ng, nb, nqb),
        in_specs=[
            pl.BlockSpec((1, hb, nrel), lambda h, b, i: (h, 0, 0), memory_space=pltpu.SMEM),
            pl.BlockSpec((hb, tq, LANE), lambda h, b, i: (h, b * nqb + i, 0)),
            pl.BlockSpec((hb, t, LANE), lambda h, b, i: (ng + h, b, 0)),
            pl.BlockSpec((hb, t, LANE), lambda h, b, i: (2 * ng + h, b, 0)),
            pl.BlockSpec((hb, tq, LANE), lambda h, b, i: (3 * ng + h, b * nqb + i, 0)),
            pl.BlockSpec((hb, m, LANE), lambda h, b, i: (ng + h, b, 0)),
            pl.BlockSpec((hb, m, LANE), lambda h, b, i: (2 * ng + h, b, 0)),
        ],
        out_specs=pl.BlockSpec((tq, hb * LANE), lambda h, b, i: (b * nqb + i, h)),
        out_shape=jax.ShapeDtypeStruct((nb * t, nh * LANE), BF16),
        scratch_shapes=[
            pltpu.VMEM((hb, 2 * WIN_H - 1, GRID_W, LANE), F32),
            pltpu.VMEM((hb, 3, nq, nk), F32),
        ],
        compiler_params=_params(("parallel", "arbitrary", "arbitrary")),
        name="na_attention",
    )(rpb.reshape(ng, hb, nrel), p, p, p, p, pc, pc)


def _ctx_attn_kernel(q_ref, k_ref, v_ref, z_ref, o_ref):
    q = (q_ref[0].astype(F32) * (LANE ** -0.5)).astype(BF16)
    s = _nt_dot(q, k_ref[0])
    m = jnp.max(s, axis=-1, keepdims=True)
    p = jnp.exp(s - m)
    l = jnp.sum(p, axis=-1, keepdims=True)
    o = _dot(p.astype(BF16), v_ref[0]) / l
    o_ref[...] = (o * _silu(z_ref[0].astype(F32))).astype(BF16)


def _ctx_attn_call(pc, nb, m, nh):
    return pl.pallas_call(
        _ctx_attn_kernel,
        grid=(nh, nb),
        in_specs=[
            pl.BlockSpec((1, m, LANE), lambda h, b: (h, b, 0)),
            pl.BlockSpec((1, m, LANE), lambda h, b: (nh + h, b, 0)),
            pl.BlockSpec((1, m, LANE), lambda h, b: (2 * nh + h, b, 0)),
            pl.BlockSpec((1, m, LANE), lambda h, b: (3 * nh + h, b, 0)),
        ],
        out_specs=pl.BlockSpec((m, LANE), lambda h, b: (b, h)),
        out_shape=jax.ShapeDtypeStruct((nb * m, nh * LANE), BF16),
        compiler_params=_params(("parallel", "parallel")),
        name="ctx_attention",
    )(pc, pc, pc, pc)


def _pool_kernel(u_ref, o_ref, *, n, cols_per_group):
    group = pl.program_id(1) // cols_per_group
    t = lax.broadcasted_iota(jnp.int32, (n, LANE), 0)

    def up(a, k):
        return jnp.where(t < n - k, pltpu.roll(a, n - k, axis=0), 0.0)

    def down(a, k):
        return jnp.where(t >= k, pltpu.roll(a, k, axis=0), 0.0)

    for gi, w in enumerate(POOL_SIZES):
        half = w // 2

        @pl.when(group == gi)
        def _(half=half):
            x = u_ref[0].astype(F32)
            fwd, bwd, k = x, x, 1
            while k < half:
                fwd = fwd + up(fwd, k)
                bwd = bwd + down(bwd, k)
                k *= 2
            total = fwd + down(bwd, 1)
            cnt = (jnp.minimum(t + half, n) - jnp.maximum(t - half, 0)).astype(F32)
            o_ref[0] = (total / cnt - x).astype(BF16)


def _pool_call(p, nb, n, ncol):
    return pl.pallas_call(
        functools.partial(_pool_kernel, n=n, cols_per_group=ncol // len(POOL_SIZES)),
        grid=(nb, ncol),
        in_specs=[pl.BlockSpec((1, n, LANE), lambda b, c: (c, b, 0))],
        out_specs=pl.BlockSpec((1, n, LANE), lambda b, c: (c, b, 0)),
        out_shape=jax.ShapeDtypeStruct((ncol, nb * n, LANE), BF16),
        compiler_params=_params(("parallel", "parallel")),
        name="pool_centre",
    )(p)


def _grp_kernel(d_ref, w_ref, z_ref, s_ref, o_ref, *, kcol, ncol):
    lhs = jnp.concatenate([d_ref[c] for c in range(kcol)], axis=-1)
    acc = _dot(lhs, w_ref[0])
    z = jnp.concatenate([z_ref[c] for c in range(ncol)], axis=-1).astype(F32)
    o_ref[...] = (acc * s_ref[...] * _silu(z)).astype(BF16)


def _grp_call(dpool, p, w_grp, scale, rows_per_batch):
    ng, cg, _ = w_grp.shape
    ncol_all, r, _ = dpool.shape
    tm = min(512, rows_per_batch)
    tn = min(512, cg)
    kcol, ncol, nj = cg // LANE, tn // LANE, cg // tn
    zoff = ncol_all // ncol
    return pl.pallas_call(
        functools.partial(_grp_kernel, kcol=kcol, ncol=ncol),
        grid=(ng, r // tm, nj),
        in_specs=[
            pl.BlockSpec((kcol, tm, LANE), lambda g, i, j: (g, i, 0)),
            pl.BlockSpec((1, cg, tn), lambda g, i, j: (g, 0, j)),
            pl.BlockSpec((ncol, tm, LANE), lambda g, i, j: (zoff + g * nj + j, i, 0)),
            pl.BlockSpec((1, tn), lambda g, i, j: (0, g * nj + j)),
        ],
        out_specs=pl.BlockSpec((tm, tn), lambda g, i, j: (i, g * nj + j)),
        out_shape=jax.ShapeDtypeStruct((r, ng * cg), BF16),
        compiler_params=_params(("parallel", "parallel", "parallel")),
        name="pool_group_matmul",
    )(dpool, w_grp, p, scale.reshape(1, ng * cg))


HGRN_LEVELS = (128, 64, 32, 16)
HGRN_DIAG = 8


def _hgrn_tri():
    t = np.arange(SLAB)
    fwd = (t[None, :] <= t[:, None]).astype(np.float32)
    tri = np.stack([np.concatenate([fwd, fwd], axis=1), np.concatenate([fwd.T, fwd.T], axis=1)])
    return jnp.asarray(tri, BF16)


def _hgrn_kernel(q_ref, af_ref, ab_ref, v_ref, z_ref, qc_ref, afc_ref, abc_ref, vc_ref, zc_ref,
                 lb_ref, gn_ref, tri_ref, u_ref, uc_ref,
                 o_scr, oi_scr, qh_scr, kb_scr, vt_scr, g_scr, qs_scr, kk_scr, b_scr, v_scr,
                 *, layer, t, m):
    nslab_c = m // SLAB
    nslab = (m + t) // SLAB

    lbs = []
    for d in (0, 1):
        raw = lb_ref[d]
        e = jnp.exp(raw - jnp.max(raw, axis=0, keepdims=True))
        probs = e / jnp.sum(e, axis=0, keepdims=True)
        lbs.append(jnp.sum(probs[1:layer + 1], axis=0, keepdims=True) if layer >= 1
                   else jnp.zeros((1, LANE), F32))

    ti = lax.broadcasted_iota(jnp.int32, (SLAB, SLAB), 0)
    si = lax.broadcasted_iota(jnp.int32, (SLAB, SLAB), 1)
    rt = lax.broadcasted_iota(jnp.int32, (SLAB, LANE), 0)
    diag_masks, level_masks = [], []
    for d in (0, 1):
        before = (si < ti) if d == 0 else (si > ti)
        diag_masks.append(((ti // HGRN_DIAG) == (si // HGRN_DIAG)) & (before | (si == ti)))
        level_masks.append({blk: ((ti // blk) == (si // blk)) & ((ti // (blk // 2)) != (si // (blk // 2)))
                            & before for blk in HGRN_LEVELS})

    def ref_delta(b, block, ref_row):
        b3 = b.reshape(SLAB // block, block, LANE)
        return (b3 - b3[:, ref_row:ref_row + 1, :]).reshape(SLAB, LANE)

    def gates_slab(srcs, r0, slab):
        row = slab * SLAB
        qs = _silu(srcs[0][0, pl.ds(r0, SLAB), :].astype(F32))
        qs_scr[pl.ds(row, SLAB), :] = qs
        v = srcs[3][0, pl.ds(r0, SLAB), :]
        v_scr[pl.ds(row, SLAB), :] = v
        vt_scr[slab] = v.astype(F32).T.astype(BF16)
        for d in (0, 1):
            a = srcs[1 + d][0, pl.ds(r0, SLAB), :].astype(F32)
            f = lbs[d] + (1.0 - lbs[d]) * (1.0 / (1.0 + jnp.exp(-a)))
            lf = jnp.log2(f)
            kk = 1.0 - f
            hi = lf.astype(BF16)
            lo = (lf - hi.astype(F32)).astype(BF16)
            b = _dot(tri_ref[d], jnp.concatenate([hi, lo], axis=0))
            b_end = b[SLAB - 1:SLAB] if d == 0 else b[0:1]
            kk_scr[d, pl.ds(row, SLAB), :] = kk
            b_scr[d, pl.ds(row, SLAB), :] = b
            qh_scr[d, pl.ds(row, SLAB), :] = (qs * jnp.exp2(b)).astype(BF16)
            kb_scr[d, pl.ds(row, SLAB), :] = (kk * jnp.exp2(b_end - b)).astype(BF16)
            g_scr[d, pl.ds(slab, 1), :] = jnp.exp2(b_end)

    def mix_slab(slab):
        row = pl.multiple_of(slab * SLAB, SLAB)
        qs = qs_scr[pl.ds(row, SLAB), :]
        v = v_scr[pl.ds(row, SLAB), :]
        atts = []
        for d in (0, 1):
            kk = kk_scr[d, pl.ds(row, SLAB), :]
            b = b_scr[d, pl.ds(row, SLAB), :]
            dlt = ref_delta(b, HGRN_DIAG, HGRN_DIAG // 2 - 1 if d == 0 else HGRN_DIAG // 2)
            qt = (qs * jnp.exp2(dlt)).astype(BF16)
            kt = (kk * jnp.exp2(-dlt)).astype(BF16)
            att = jnp.where(diag_masks[d], _nt_dot(qt, kt), 0.0)
            for block in HGRN_LEVELS:
                half = block // 2
                dlt = ref_delta(b, block, half - 1 if d == 0 else half)
                pos = rt & (block - 1)
                is_target = (pos >= half) if d == 0 else (pos < half)
                x = (jnp.where(is_target, qs, kk) * jnp.exp2(-jnp.abs(dlt))).astype(BF16)
                att = jnp.where(level_masks[d][block], _nt_dot(x, x), att)
            atts.append(att.astype(BF16))
        o_scr[pl.ds(row, SLAB), :] = _dot(jnp.concatenate(atts, axis=1), jnp.concatenate([v, v], axis=0))

    csrc = (qc_ref, afc_ref, abc_ref, vc_ref)
    lsrc = (q_ref, af_ref, ab_ref, v_ref)
    for s in range(nslab_c):
        gates_slab(csrc, s * SLAB, s)

    def gates_body(s, carry):
        gates_slab(lsrc, pl.multiple_of(s * SLAB, SLAB), s + nslab_c)
        return carry

    lax.fori_loop(0, t // SLAB, gates_body, 0, unroll=2)

    def mix_body(s, carry):
        mix_slab(s)
        return carry

    lax.fori_loop(0, nslab, mix_body, 0, unroll=2)

    def serial(i, carry):
        slabs = (i, jnp.where(i < nslab_c, nslab_c - 1 - i, nslab + nslab_c - 1 - i))
        new = []
        for d in (0, 1):
            slab, st = slabs[d], carry[d]
            row = pl.multiple_of(slab * SLAB, SLAB)
            oi_scr[d, pl.ds(row, SLAB), :] = _nt_dot(qh_scr[d, pl.ds(row, SLAB), :], st.astype(BF16))
            ds = _dot(vt_scr[slab], kb_scr[d, pl.ds(row, SLAB), :])
            new.append(g_scr[d, pl.ds(slab, 1), :] * st + ds)
        return tuple(new)

    zero = jnp.zeros((LANE, LANE), F32)
    lax.fori_loop(0, nslab, serial, (zero, zero), unroll=2)

    def finish(r0, n, z_blk, out_ref):
        o = o_scr[r0:r0 + n, :] + oi_scr[0, r0:r0 + n, :] + oi_scr[1, r0:r0 + n, :]
        ms = jnp.mean(o * o, axis=-1, keepdims=True)
        on = o * lax.rsqrt(ms + EPS) * gn_ref[...]
        out_ref[...] = (on * _silu(z_blk.astype(F32))).astype(BF16)

    finish(0, m, zc_ref[0], uc_ref)
    finish(m, t, z_ref[0], u_ref)


def _hgrn_call(p, pc, lb_raw, g_norm, layer, nb, t, m, nh):
    depth = lb_raw.shape[1]
    tt = m + t
    assert t % (2 * SLAB) == 0 and m % SLAB == 0 and (tt // SLAB) % 2 == 0
    lat = lambda k: pl.BlockSpec((1, t, LANE), lambda b, h, k=k: (k * nh + h, b, 0))
    ctx = lambda k: pl.BlockSpec((1, m, LANE), lambda b, h, k=k: (k * nh + h, b, 0))
    return pl.pallas_call(
        functools.partial(_hgrn_kernel, layer=layer, t=t, m=m),
        grid=(nb, nh),
        in_specs=[lat(k) for k in range(5)] + [ctx(k) for k in range(5)] + [
            pl.BlockSpec((2, depth, LANE), lambda b, h: (0, 0, h)),
            pl.BlockSpec((1, LANE), lambda b, h: (0, h)),
            pl.BlockSpec((2, SLAB, 2 * SLAB), lambda b, h: (0, 0, 0)),
        ],
        out_specs=[
            pl.BlockSpec((t, LANE), lambda b, h: (b, h)),
            pl.BlockSpec((m, LANE), lambda b, h: (b, h)),
        ],
        out_shape=[
            jax.ShapeDtypeStruct((nb * t, nh * LANE), BF16),
            jax.ShapeDtypeStruct((nb * m, nh * LANE), BF16),
        ],
        scratch_shapes=[
            pltpu.VMEM((tt, LANE), F32),
            pltpu.VMEM((2, tt, LANE), F32),
            pltpu.VMEM((2, tt, LANE), BF16),
            pltpu.VMEM((2, tt, LANE), BF16),
            pltpu.VMEM((tt // SLAB, LANE, SLAB), BF16),
            pltpu.VMEM((2, tt // SLAB, LANE), F32),
            pltpu.VMEM((tt, LANE), F32),
            pltpu.VMEM((2, tt, LANE), F32),
            pltpu.VMEM((2, tt, LANE), F32),
            pltpu.VMEM((tt, LANE), BF16),
        ],
        compiler_params=_params(("parallel", "parallel")),
        name="hgrn_scan",
    )(p, p, p, p, p, pc, pc, pc, pc, pc, lb_raw, g_norm.reshape(1, nh * LANE), _hgrn_tri())


def kernel(x, c, ctx, c_ctx, w_mod, b_mod, g_pre, g_post, na_w_in, na_rpb, na_w_out,
           pool_w_in, pool_w_grp, pool_scale, pool_w_out, hgrn_w_in, hgrn_lb, hgrn_gnorm, hgrn_w_out):
    nb, t, d = x.shape
    m = ctx.shape[1]
    depth = w_mod.shape[0]
    nh = na_w_out.shape[1] // LANE
    rows = t // GRID_W

    mods = _mod_call(c, c_ctx, w_mod, b_mod)
    xl = x.reshape(nb * t, d)
    xc = ctx.reshape(nb * m, d)

    for i in range(depth):
        kind, j = i % 3, i // 3
        need_ctx = i < depth - 1
        mod = mods[i]
        shift, scale, gate = (mod[:nb, None, k * d:(k + 1) * d] for k in range(3))
        shift_c, scale_c, gate_c = (mod[nb:nb + 1, None, k * d:(k + 1) * d] for k in range(3))
        if kind == 0:
            w_in, w_out = na_w_in[j], na_w_out[j]
        elif kind == 1:
            w_in, w_out = pool_w_in[j], pool_w_out[j]
        else:
            w_in, w_out = hgrn_w_in[j], hgrn_w_out[j]
        w_in = w_in.astype(BF16)
        w_out = w_out.astype(BF16)

        p = _proj_call(xl, g_pre[i], scale, shift, w_in, t, f"proj_lat_{i}")
        pc = _proj_call(xc, g_pre[i], scale_c, shift_c, w_in, nb * m, f"proj_ctx_{i}")

        uc = None
        if kind == 0:
            u = _na_call(p, pc, na_rpb[j], nb, t, m, nh)
            if need_ctx:
                uc = _ctx_attn_call(pc, nb, m, nh)
        elif kind == 1:
            ncol = w_out.shape[0] // LANE
            wg = pool_w_grp[j].astype(BF16)
            u = _grp_call(_pool_call(p, nb, t, ncol), p, wg, pool_scale[j], t)
            if need_ctx:
                uc = _grp_call(_pool_call(pc, nb, m, ncol), pc, wg, pool_scale[j], m)
        else:
            u, uc = _hgrn_call(p, pc, hgrn_lb, hgrn_gnorm[j], i, nb, t, m, nh)

        xl = _out_call(u, w_out, xl, gate, g_post[i], t, f"out_lat_{i}")
        if need_ctx:
            xc = _out_call(uc, w_out, xc, gate_c, g_post[i], nb * m, f"out_ctx_{i}")

    return xl.reshape(nb, t, d)
```

```python
import functools

import numpy as np
import jax
import jax.numpy as jnp
from jax import lax
from jax.experimental import pallas as pl
from jax.experimental.pallas import tpu as pltpu

F32 = jnp.float32
BF16 = jnp.bfloat16

LANE = 128
VMEM_LIMIT = 56 * 1024 * 1024

EPS = 1e-6
GRID_W = 64
WIN_H = 8
WIN_W = 16
POOL_SIZES = (2, 4, 8, 16)
NEG = -1e30

NA_QROWS = 4
NA_HEADS = 2
LOG2E = 1.4426950408889634
NA_BAND = NA_QROWS + WIN_H - 1
SLAB = 128
PROJ_TM, PROJ_TN = 1024, 512
OUT_TM = 512


def _silu(x):
    return x / (1.0 + jnp.exp(-x))


def _nt_dot(a, b):
    return lax.dot_general(a, b, (((1,), (1,)), ((), ())), preferred_element_type=F32)


def _dot(a, b):
    return jnp.dot(a, b, preferred_element_type=F32)


def _params(sem, vmem=VMEM_LIMIT):
    return pltpu.CompilerParams(dimension_semantics=sem, vmem_limit_bytes=vmem)


def _mod_kernel(c_ref, w_ref, b_ref, o_ref):
    s = _silu(c_ref[...])
    s_hi = s.astype(BF16)
    s_lo = (s - s_hi.astype(F32)).astype(BF16)
    w = w_ref[0]
    w_hi = w.astype(BF16)
    w_lo = (w - w_hi.astype(F32)).astype(BF16)
    acc = _dot(s_hi, w_hi) + _dot(s_lo, w_hi) + _dot(s_hi, w_lo)
    o_ref[0] = acc + b_ref[0]


def _mod_call(c, c_ctx, w_mod, b_mod):
    depth, d, n3 = w_mod.shape
    nb = c.shape[0]
    assert nb + 1 <= 8
    cs = jnp.zeros((8, d), F32).at[:nb].set(c).at[nb].set(c_ctx)
    tn = 1024 if n3 % 1024 == 0 else n3
    return pl.pallas_call(
        _mod_kernel,
        grid=(depth, n3 // tn),
        in_specs=[
            pl.BlockSpec((8, d), lambda l, j: (0, 0)),
            pl.BlockSpec((1, d, tn), lambda l, j: (l, 0, j)),
            pl.BlockSpec((1, 1, tn), lambda l, j: (l, 0, j)),
        ],
        out_specs=pl.BlockSpec((1, 8, tn), lambda l, j: (l, 0, j)),
        out_shape=jax.ShapeDtypeStruct((depth, 8, n3), F32),
        compiler_params=_params(("parallel", "parallel")),
        name="mod_vectors",
    )(cs, w_mod, b_mod.reshape(depth, 1, n3))


def _proj_kernel(x_ref, g_ref, sc_ref, sh_ref, w_ref, o_ref, h_ref, *, ncol):
    @pl.when(pl.program_id(1) == 0)
    def _():
        x = x_ref[...]
        ms = jnp.mean(x * x, axis=-1, keepdims=True)
        y = x * lax.rsqrt(ms + EPS) * g_ref[...]
        h_ref[...] = (y * (1.0 + sc_ref[0]) + sh_ref[0]).astype(BF16)

    acc = _dot(h_ref[...], w_ref[...].astype(BF16))
    for c in range(ncol):
        o_ref[c] = acc[:, c * LANE:(c + 1) * LANE].astype(BF16)


def _proj_call(x2d, g, scale, shift, w, rows_per_batch, name):
    r, d = x2d.shape
    n = w.shape[1]
    tm = min(PROJ_TM, rows_per_batch)
    tn = PROJ_TN if n % PROJ_TN == 0 else n
    assert r % tm == 0 and rows_per_batch % tm == 0 and n % tn == 0 and tn % LANE == 0
    ncol = tn // LANE
    bidx = lambda i, j: ((i * tm) // rows_per_batch, 0, 0)
    return pl.pallas_call(
        functools.partial(_proj_kernel, ncol=ncol),
        grid=(r // tm, n // tn),
        in_specs=[
            pl.BlockSpec((tm, d), lambda i, j: (i, 0)),
            pl.BlockSpec((1, d), lambda i, j: (0, 0)),
            pl.BlockSpec((1, 1, d), bidx),
            pl.BlockSpec((1, 1, d), bidx),
            pl.BlockSpec((d, tn), lambda i, j: (0, j)),
        ],
        out_specs=pl.BlockSpec((ncol, tm, LANE), lambda i, j: (j, i, 0)),
        out_shape=jax.ShapeDtypeStruct((n // LANE, r, LANE), BF16),
        scratch_shapes=[pltpu.VMEM((tm, d), BF16)],
        compiler_params=_params(("parallel", "arbitrary")),
        name=name,
    )(x2d, g.reshape(1, d), scale, shift, w)


def _out_kernel(u_ref, w_ref, x_ref, gate_ref, gp_ref, o_ref):
    y = _dot(u_ref[...], w_ref[...])
    ms = jnp.mean(y * y, axis=-1, keepdims=True)
    r = y * lax.rsqrt(ms + EPS) * gp_ref[...]
    o_ref[...] = x_ref[...] + gate_ref[0] * r


def _out_call(u, w, x2d, gate, g_post, rows_per_batch, name):
    r, kdim = u.shape
    d = w.shape[1]
    tm = min(OUT_TM, rows_per_batch)
    assert r % tm == 0 and rows_per_batch % tm == 0
    return pl.pallas_call(
        _out_kernel,
        grid=(r // tm,),
        in_specs=[
            pl.BlockSpec((tm, kdim), lambda i: (i, 0)),
            pl.BlockSpec((kdim, d), lambda i: (0, 0), pipeline_mode=pl.Buffered(1)),
            pl.BlockSpec((tm, d), lambda i: (i, 0)),
            pl.BlockSpec((1, 1, d), lambda i: ((i * tm) // rows_per_batch, 0, 0)),
            pl.BlockSpec((1, d), lambda i: (0, 0)),
        ],
        out_specs=pl.BlockSpec((tm, d), lambda i: (i, 0)),
        out_shape=jax.ShapeDtypeStruct((r, d), F32),
        compiler_params=_params(("parallel",)),
        name=name,
    )(u, w, x2d, gate, g_post.reshape(1, d))


def _na_block_plan(rows):
    plans = []
    for qs, bs in ((0, 0), (NA_QROWS, 0), (rows - NA_QROWS, rows - NA_BAND)):
        rq = qs + np.arange(NA_QROWS)
        rk = bs + np.arange(NA_BAND)
        r0 = np.clip(rq - WIN_H // 2, 0, rows - WIN_H)
        row_in = (rk[None, :] >= r0[:, None]) & (rk[None, :] < r0[:, None] + WIN_H)
        plans.append(np.where(row_in, rk[None, :] - rq[:, None] + (WIN_H - 1), -1))
    return plans


def _na_build_bias(rpb_ref, hh, tb_scr, bias_scr, rows):
    nrel_w = 2 * WIN_W - 1
    lane = lax.broadcasted_iota(jnp.int32, (GRID_W, LANE), 1)
    cq = lax.broadcasted_iota(jnp.int32, (GRID_W, LANE), 0)
    ck = lane & (GRID_W - 1)
    didx = ck - cq + (WIN_W - 1)
    c0 = jnp.clip(cq - WIN_W // 2, 0, GRID_W - WIN_W)
    col_in = (ck >= c0) & (ck < c0 + WIN_W)
    for dr in range(2 * WIN_H - 1):
        tile = jnp.zeros((GRID_W, LANE), F32)
        for j in range(nrel_w):
            tile = jnp.where(didx == j, rpb_ref[0, hh, dr * nrel_w + j] * LOG2E, tile)
        tb_scr[hh, dr] = jnp.where(col_in, tile, NEG)
    neg = jnp.full((GRID_W, LANE), NEG, F32)
    for var, plan in enumerate(_na_block_plan(rows)):
        for rq in range(NA_QROWS):
            for pair in range((NA_BAND + 1) // 2):
                rks = [rk for rk in (2 * pair, 2 * pair + 1) if rk < NA_BAND]
                src = [tb_scr[hh, int(plan[rq, rk])] if plan[rq, rk] >= 0 else neg for rk in rks]
                tile = src[0] if len(src) == 1 else jnp.where(lane < GRID_W, src[0], src[1])
                width = GRID_W * len(rks)
                bias_scr[hh, var, rq * GRID_W:(rq + 1) * GRID_W,
                         pair * LANE:pair * LANE + width] = tile[:, :width]


def _na_kernel(rpb_ref, q_ref, k_ref, v_ref, z_ref, kc_ref, vc_ref, o_ref, tb_scr, bias_scr,
               *, rows, heads):
    qb = pl.program_id(2)
    nqb = rows // NA_QROWS

    @pl.when((pl.program_id(1) == 0) & (qb == 0))
    def _():
        for hh in range(heads):
            _na_build_bias(rpb_ref, hh, tb_scr, bias_scr, rows)

    var = jnp.where(qb == 0, 0, jnp.where(qb == nqb - 1, 2, 1))
    rs = jnp.clip(qb * NA_QROWS - WIN_H // 2, 0, rows - NA_BAND)
    start = pl.multiple_of(rs * GRID_W, GRID_W)
    nkeys = NA_BAND * GRID_W
    for hh in range(heads):
        q = (q_ref[hh].astype(F32) * (LANE ** -0.5 * LOG2E)).astype(BF16)
        kb = k_ref[hh, pl.ds(start, nkeys), :]
        vb = v_ref[hh, pl.ds(start, nkeys), :]
        s1 = _nt_dot(q, kb) + bias_scr[hh, var]
        s2 = _nt_dot(q, kc_ref[hh])
        mx = jnp.maximum(jnp.max(s1, axis=-1, keepdims=True), jnp.max(s2, axis=-1, keepdims=True))
        p1 = jnp.exp2(s1 - mx)
        p2 = jnp.exp2(s2 - mx)
        l = jnp.sum(p1, axis=-1, keepdims=True) + jnp.sum(p2, axis=-1, keepdims=True)
        o = (_dot(p1.astype(BF16), vb) + _dot(p2.astype(BF16), vc_ref[hh])) / l
        o_ref[:, hh * LANE:(hh + 1) * LANE] = (o * _silu(z_ref[hh].astype(F32))).astype(BF16)


def _na_call(p, pc, rpb, nb, t, m, nh):
    rows = t // GRID_W
    tq = NA_QROWS * GRID_W
    nqb = t // tq
    nq, nk = tq, NA_BAND * GRID_W
    nrel = rpb.shape[1] * rpb.shape[2]
    hb = NA_HEADS if nh % NA_HEADS == 0 else 1
    ng = nh // hb
    return pl.pallas_call(
        functools.partial(_na_kernel, rows=rows, heads=hb),
        grid=(ng, nb, nqb),
        in_specs=[
            pl.BlockSpec((1, hb, nrel), lambda h, b, i: (h, 0, 0), memory_space=pltpu.SMEM),
            pl.BlockSpec((hb, tq, LANE), lambda h, b, i: (h, b * nqb + i, 0)),
            pl.BlockSpec((hb, t, LANE), lambda h, b, i: (ng + h, b, 0)),
            pl.BlockSpec((hb, t, LANE), lambda h, b, i: (2 * ng + h, b, 0)),
            pl.BlockSpec((hb, tq, LANE), lambda h, b, i: (3 * ng + h, b * nqb + i, 0)),
            pl.BlockSpec((hb, m, LANE), lambda h, b, i: (ng + h, b, 0)),
            pl.BlockSpec((hb, m, LANE), lambda h, b, i: (2 * ng + h, b, 0)),
        ],
        out_specs=pl.BlockSpec((tq, hb * LANE), lambda h, b, i: (b * nqb + i, h)),
        out_shape=jax.ShapeDtypeStruct((nb * t, nh * LANE), BF16),
        scratch_shapes=[
            pltpu.VMEM((hb, 2 * WIN_H - 1, GRID_W, LANE), F32),
            pltpu.VMEM((hb, 3, nq, nk), F32),
        ],
        compiler_params=_params(("parallel", "arbitrary", "arbitrary")),
        name="na_attention",
    )(rpb.reshape(ng, hb, nrel), p, p, p, p, pc, pc)


def _ctx_attn_kernel(q_ref, k_ref, v_ref, z_ref, o_ref):
    q = (q_ref[0].astype(F32) * (LANE ** -0.5)).astype(BF16)
    s = _nt_dot(q, k_ref[0])
    m = jnp.max(s, axis=-1, keepdims=True)
    p = jnp.exp(s - m)
    l = jnp.sum(p, axis=-1, keepdims=True)
    o = _dot(p.astype(BF16), v_ref[0]) / l
    o_ref[...] = (o * _silu(z_ref[0].astype(F32))).astype(BF16)


def _ctx_attn_call(pc, nb, m, nh):
    return pl.pallas_call(
        _ctx_attn_kernel,
        grid=(nh, nb),
        in_specs=[
            pl.BlockSpec((1, m, LANE), lambda h, b: (h, b, 0)),
            pl.BlockSpec((1, m, LANE), lambda h, b: (nh + h, b, 0)),
            pl.BlockSpec((1, m, LANE), lambda h, b: (2 * nh + h, b, 0)),
            pl.BlockSpec((1, m, LANE), lambda h, b: (3 * nh + h, b, 0)),
        ],
        out_specs=pl.BlockSpec((m, LANE), lambda h, b: (b, h)),
        out_shape=jax.ShapeDtypeStruct((nb * m, nh * LANE), BF16),
        compiler_params=_params(("parallel", "parallel")),
        name="ctx_attention",
    )(pc, pc, pc, pc)


def _pool_kernel(u_ref, o_ref, *, n, cols_per_group):
    group = pl.program_id(1) // cols_per_group
    t = lax.broadcasted_iota(jnp.int32, (n, LANE), 0)

    def up(a, k):
        return jnp.where(t < n - k, pltpu.roll(a, n - k, axis=0), 0.0)

    def down(a, k):
        return jnp.where(t >= k, pltpu.roll(a, k, axis=0), 0.0)

    for gi, w in enumerate(POOL_SIZES):
        half = w // 2

        @pl.when(group == gi)
        def _(half=half):
            x = u_ref[0].astype(F32)
            fwd, bwd, k = x, x, 1
            while k < half:
                fwd = fwd + up(fwd, k)
                bwd = bwd + down(bwd, k)
                k *= 2
            total = fwd + down(bwd, 1)
            cnt = (jnp.minimum(t + half, n) - jnp.maximum(t - half, 0)).astype(F32)
            o_ref[0] = (total / cnt - x).astype(BF16)


def _pool_call(p, nb, n, ncol):
    return pl.pallas_call(
        functools.partial(_pool_kernel, n=n, cols_per_group=ncol // len(POOL_SIZES)),
        grid=(nb, ncol),
        in_specs=[pl.BlockSpec((1, n, LANE), lambda b, c: (c, b, 0))],
        out_specs=pl.BlockSpec((1, n, LANE), lambda b, c: (c, b, 0)),
        out_shape=jax.ShapeDtypeStruct((ncol, nb * n, LANE), BF16),
        compiler_params=_params(("parallel", "parallel")),
        name="pool_centre",
    )(p)


def _grp_kernel(d_ref, w_ref, z_ref, s_ref, o_ref, *, kcol, ncol):
    lhs = jnp.concatenate([d_ref[c] for c in range(kcol)], axis=-1)
    acc = _dot(lhs, w_ref[0])
    z = jnp.concatenate([z_ref[c] for c in range(ncol)], axis=-1).astype(F32)
    o_ref[...] = (acc * s_ref[...] * _silu(z)).astype(BF16)


def _grp_call(dpool, p, w_grp, scale, rows_per_batch):
    ng, cg, _ = w_grp.shape
    ncol_all, r, _ = dpool.shape
    tm = min(512, rows_per_batch)
    tn = min(512, cg)
    kcol, ncol, nj = cg // LANE, tn // LANE, cg // tn
    zoff = ncol_all // ncol
    return pl.pallas_call(
        functools.partial(_grp_kernel, kcol=kcol, ncol=ncol),
        grid=(ng, r // tm, nj),
        in_specs=[
            pl.BlockSpec((kcol, tm, LANE), lambda g, i, j: (g, i, 0)),
            pl.BlockSpec((1, cg, tn), lambda g, i, j: (g, 0, j)),
            pl.BlockSpec((ncol, tm, LANE), lambda g, i, j: (zoff + g * nj + j, i, 0)),
            pl.BlockSpec((1, tn), lambda g, i, j: (0, g * nj + j)),
        ],
        out_specs=pl.BlockSpec((tm, tn), lambda g, i, j: (i, g * nj + j)),
        out_shape=jax.ShapeDtypeStruct((r, ng * cg), BF16),
        compiler_params=_params(("parallel", "parallel", "parallel")),
        name="pool_group_matmul",
    )(dpool, w_grp, p, scale.reshape(1, ng * cg))


HGRN_LEVELS = (128, 64, 32, 16)
HGRN_DIAG = 8


def _hgrn_tri():
    t = np.arange(SLAB)
    fwd = (t[None, :] <= t[:, None]).astype(np.float32)
    tri = np.stack([np.concatenate([fwd, fwd], axis=1), np.concatenate([fwd.T, fwd.T], axis=1)])
    return jnp.asarray(tri, BF16)


def _hgrn_kernel(q_ref, af_ref, ab_ref, v_ref, z_ref, qc_ref, afc_ref, abc_ref, vc_ref, zc_ref,
                 lb_ref, gn_ref, tri_ref, u_ref, uc_ref,
                 o_scr, qh_scr, kb_scr, vt_scr, g_scr, qs_scr, kk_scr, b_scr, v_scr, ds_scr, sin_scr,
                 *, layer, t, m):
    nslab_c = m // SLAB
    nslab = (m + t) // SLAB

    lbs = []
    for d in (0, 1):
        raw = lb_ref[d]
        e = jnp.exp(raw - jnp.max(raw, axis=0, keepdims=True))
        probs = e / jnp.sum(e, axis=0, keepdims=True)
        lbs.append(jnp.sum(probs[1:layer + 1], axis=0, keepdims=True) if layer >= 1
                   else jnp.zeros((1, LANE), F32))

    ti = lax.broadcasted_iota(jnp.int32, (SLAB, SLAB), 0)
    si = lax.broadcasted_iota(jnp.int32, (SLAB, SLAB), 1)
    rt = lax.broadcasted_iota(jnp.int32, (SLAB, LANE), 0)
    diag_masks, level_masks = [], []
    for d in (0, 1):
        before = (si < ti) if d == 0 else (si > ti)
        diag_masks.append(((ti // HGRN_DIAG) == (si // HGRN_DIAG)) & (before | (si == ti)))
        level_masks.append({blk: ((ti // blk) == (si // blk)) & ((ti // (blk // 2)) != (si // (blk // 2)))
                            & before for blk in HGRN_LEVELS})

    def ref_delta(b, block, ref_row):
        b3 = b.reshape(SLAB // block, block, LANE)
        return (b3 - b3[:, ref_row:ref_row + 1, :]).reshape(SLAB, LANE)

    def gates_slab(srcs, r0, slab):
        row = slab * SLAB
        qs = _silu(srcs[0][0, pl.ds(r0, SLAB), :].astype(F32))
        qs_scr[pl.ds(row, SLAB), :] = qs
        v = srcs[3][0, pl.ds(r0, SLAB), :]
        v_scr[pl.ds(row, SLAB), :] = v
        vt_scr[slab] = v.astype(F32).T.astype(BF16)
        for d in (0, 1):
            a = srcs[1 + d][0, pl.ds(r0, SLAB), :].astype(F32)
            f = lbs[d] + (1.0 - lbs[d]) * (1.0 / (1.0 + jnp.exp(-a)))
            lf = jnp.log2(f)
            kk = 1.0 - f
            hi = lf.astype(BF16)
            lo = (lf - hi.astype(F32)).astype(BF16)
            b = _dot(tri_ref[d], jnp.concatenate([hi, lo], axis=0))
            b_end = b[SLAB - 1:SLAB] if d == 0 else b[0:1]
            kk_scr[d, pl.ds(row, SLAB), :] = kk
            b_scr[d, pl.ds(row, SLAB), :] = b
            qh_scr[d, pl.ds(row, SLAB), :] = (qs * jnp.exp2(b)).astype(BF16)
            kb_scr[d, pl.ds(row, SLAB), :] = (kk * jnp.exp2(b_end - b)).astype(BF16)
            g_scr[d, pl.ds(slab, 1), :] = jnp.exp2(b_end)

    def mix_slab(slab):
        row = pl.multiple_of(slab * SLAB, SLAB)
        qs = qs_scr[pl.ds(row, SLAB), :]
        v = v_scr[pl.ds(row, SLAB), :]
        atts = []
        for d in (0, 1):
            kk = kk_scr[d, pl.ds(row, SLAB), :]
            b = b_scr[d, pl.ds(row, SLAB), :]
            dlt = ref_delta(b, HGRN_DIAG, HGRN_DIAG // 2 - 1 if d == 0 else HGRN_DIAG // 2)
            qt = (qs * jnp.exp2(dlt)).astype(BF16)
            kt = (kk * jnp.exp2(-dlt)).astype(BF16)
            att = jnp.where(diag_masks[d], _nt_dot(qt, kt), 0.0)
            for block in HGRN_LEVELS:
                half = block // 2
                dlt = ref_delta(b, block, half - 1 if d == 0 else half)
                pos = rt & (block - 1)
                is_target = (pos >= half) if d == 0 else (pos < half)
                x = (jnp.where(is_target, qs, kk) * jnp.exp2(-jnp.abs(dlt))).astype(BF16)
                att = jnp.where(level_masks[d][block], _nt_dot(x, x), att)
            atts.append(att.astype(BF16))
        o = _dot(jnp.concatenate(atts, axis=1), jnp.concatenate([v, v], axis=0))
        for d in (0, 1):
            o = o + _nt_dot(qh_scr[d, pl.ds(row, SLAB), :], sin_scr[d, slab])
        o_scr[pl.ds(row, SLAB), :] = o

    csrc = (qc_ref, afc_ref, abc_ref, vc_ref)
    lsrc = (q_ref, af_ref, ab_ref, v_ref)
    for s in range(nslab_c):
        gates_slab(csrc, s * SLAB, s)

    def gates_body(s, carry):
        gates_slab(lsrc, pl.multiple_of(s * SLAB, SLAB), s + nslab_c)
        return carry

    lax.fori_loop(0, t // SLAB, gates_body, 0, unroll=2)

    def ds_body(s, carry):
        row = pl.multiple_of(s * SLAB, SLAB)
        for d in (0, 1):
            ds_scr[d, s] = _dot(vt_scr[s], kb_scr[d, pl.ds(row, SLAB), :])
        return carry

    lax.fori_loop(0, nslab, ds_body, 0, unroll=3)

    orders = (list(range(nslab)),
              list(range(nslab_c - 1, -1, -1)) + list(range(nslab - 1, nslab_c - 1, -1)))
    for d in (0, 1):
        st = jnp.zeros((LANE, LANE), F32)
        for slab in orders[d]:
            sin_scr[d, slab] = st.astype(BF16)
            st = g_scr[d, slab:slab + 1, :] * st + ds_scr[d, slab]

    def mix_body(s, carry):
        mix_slab(s)
        return carry

    lax.fori_loop(0, nslab, mix_body, 0, unroll=3)

    def finish(r0, n, z_blk, out_ref):
        o = o_scr[r0:r0 + n, :]
        ms = jnp.mean(o * o, axis=-1, keepdims=True)
        on = o * lax.rsqrt(ms + EPS) * gn_ref[...]
        out_ref[...] = (on * _silu(z_blk.astype(F32))).astype(BF16)

    finish(0, m, zc_ref[0], uc_ref)
    finish(m, t, z_ref[0], u_ref)


def _hgrn_call(p, pc, lb_raw, g_norm, layer, nb, t, m, nh):
    depth = lb_raw.shape[1]
    tt = m + t
    assert t % (2 * SLAB) == 0 and m % SLAB == 0 and (tt // SLAB) % 3 == 0
    lat = lambda k: pl.BlockSpec((1, t, LANE), lambda b, h, k=k: (k * nh + h, b, 0))
    ctx = lambda k: pl.BlockSpec((1, m, LANE), lambda b, h, k=k: (k * nh + h, b, 0))
    return pl.pallas_call(
        functools.partial(_hgrn_kernel, layer=layer, t=t, m=m),
        grid=(nb, nh),
        in_specs=[lat(k) for k in range(5)] + [ctx(k) for k in range(5)] + [
            pl.BlockSpec((2, depth, LANE), lambda b, h: (0, 0, h)),
            pl.BlockSpec((1, LANE), lambda b, h: (0, h)),
            pl.BlockSpec((2, SLAB, 2 * SLAB), lambda b, h: (0, 0, 0)),
        ],
        out_specs=[
            pl.BlockSpec((t, LANE), lambda b, h: (b, h)),
            pl.BlockSpec((m, LANE), lambda b, h: (b, h)),
        ],
        out_shape=[
            jax.ShapeDtypeStruct((nb * t, nh * LANE), BF16),
            jax.ShapeDtypeStruct((nb * m, nh * LANE), BF16),
        ],
        scratch_shapes=[
            pltpu.VMEM((tt, LANE), F32),
            pltpu.VMEM((2, tt, LANE), BF16),
            pltpu.VMEM((2, tt, LANE), BF16),
            pltpu.VMEM((tt // SLAB, LANE, SLAB), BF16),
            pltpu.VMEM((2, tt // SLAB, LANE), F32),
            pltpu.VMEM((tt, LANE), F32),
            pltpu.VMEM((2, tt, LANE), F32),
            pltpu.VMEM((2, tt, LANE), F32),
            pltpu.VMEM((tt, LANE), BF16),
            pltpu.VMEM((2, tt // SLAB, LANE, LANE), F32),
            pltpu.VMEM((2, tt // SLAB, LANE, LANE), BF16),
        ],
        compiler_params=_params(("parallel", "parallel")),
        name="hgrn_scan",
    )(p, p, p, p, p, pc, pc, pc, pc, pc, lb_raw, g_norm.reshape(1, nh * LANE), _hgrn_tri())


def kernel(x, c, ctx, c_ctx, w_mod, b_mod, g_pre, g_post, na_w_in, na_rpb, na_w_out,
           pool_w_in, pool_w_grp, pool_scale, pool_w_out, hgrn_w_in, hgrn_lb, hgrn_gnorm, hgrn_w_out):
    nb, t, d = x.shape
    m = ctx.shape[1]
    depth = w_mod.shape[0]
    nh = na_w_out.shape[1] // LANE
    rows = t // GRID_W

    mods = _mod_call(c, c_ctx, w_mod, b_mod)
    xl = x.reshape(nb * t, d)
    xc = ctx.reshape(nb * m, d)

    for i in range(depth):
        kind, j = i % 3, i // 3
        need_ctx = i < depth - 1
        mod = mods[i]
        shift, scale, gate = (mod[:nb, None, k * d:(k + 1) * d] for k in range(3))
        shift_c, scale_c, gate_c = (mod[nb:nb + 1, None, k * d:(k + 1) * d] for k in range(3))
        if kind == 0:
            w_in, w_out = na_w_in[j], na_w_out[j]
        elif kind == 1:
            w_in, w_out = pool_w_in[j], pool_w_out[j]
        else:
            w_in, w_out = hgrn_w_in[j], hgrn_w_out[j]
        w_out = w_out.astype(BF16)

        p = _proj_call(xl, g_pre[i], scale, shift, w_in, t, f"proj_lat_{i}")
        pc = _proj_call(xc, g_pre[i], scale_c, shift_c, w_in, nb * m, f"proj_ctx_{i}")

        uc = None
        if kind == 0:
            u = _na_call(p, pc, na_rpb[j], nb, t, m, nh)
            if need_ctx:
                uc = _ctx_attn_call(pc, nb, m, nh)
        elif kind == 1:
            ncol = w_out.shape[0] // LANE
            wg = pool_w_grp[j].astype(BF16)
            u = _grp_call(_pool_call(p, nb, t, ncol), p, wg, pool_scale[j], t)
            if need_ctx:
                uc = _grp_call(_pool_call(pc, nb, m, ncol), pc, wg, pool_scale[j], m)
        else:
            u, uc = _hgrn_call(p, pc, hgrn_lb, hgrn_gnorm[j], i, nb, t, m, nh)

        xl = _out_call(u, w_out, xl, gate, g_post[i], t, f"out_lat_{i}")
        if need_ctx:
            xc = _out_call(uc, w_out, xc, gate_c, g_post[i], nb * m, f"out_ctx_{i}")

    return xl.reshape(nb, t, d)
```

```python
import functools

import numpy as np
import jax
import jax.numpy as jnp
from jax import lax
from jax.experimental import pallas as pl
from jax.experimental.pallas import tpu as pltpu

F32 = jnp.float32
BF16 = jnp.bfloat16

LANE = 128
VMEM_LIMIT = 56 * 1024 * 1024

EPS = 1e-6
GRID_W = 64
WIN_H = 8
WIN_W = 16
POOL_SIZES = (2, 4, 8, 16)
NEG = -1e30

NA_QROWS = 4
NA_HEADS = 4
LOG2E = 1.4426950408889634
NA_BAND = NA_QROWS + WIN_H - 1
SLAB = 128
PROJ_TM, PROJ_TN = 1024, 512
OUT_TM = 512


def _silu(x):
    return x / (1.0 + jnp.exp(-x))


def _nt_dot(a, b):
    return lax.dot_general(a, b, (((1,), (1,)), ((), ())), preferred_element_type=F32)


def _dot(a, b):
    return jnp.dot(a, b, preferred_element_type=F32)


def _params(sem, vmem=VMEM_LIMIT):
    return pltpu.CompilerParams(dimension_semantics=sem, vmem_limit_bytes=vmem)


def _mod_kernel(c_ref, w_ref, b_ref, o_ref):
    s = _silu(c_ref[...])
    s_hi = s.astype(BF16)
    s_lo = (s - s_hi.astype(F32)).astype(BF16)
    w = w_ref[0]
    w_hi = w.astype(BF16)
    w_lo = (w - w_hi.astype(F32)).astype(BF16)
    acc = _dot(s_hi, w_hi) + _dot(s_lo, w_hi) + _dot(s_hi, w_lo)
    o_ref[0] = acc + b_ref[0]


def _mod_call(c, c_ctx, w_mod, b_mod):
    depth, d, n3 = w_mod.shape
    nb = c.shape[0]
    assert nb + 1 <= 8
    cs = jnp.zeros((8, d), F32).at[:nb].set(c).at[nb].set(c_ctx)
    tn = 1024 if n3 % 1024 == 0 else n3
    return pl.pallas_call(
        _mod_kernel,
        grid=(depth, n3 // tn),
        in_specs=[
            pl.BlockSpec((8, d), lambda l, j: (0, 0)),
            pl.BlockSpec((1, d, tn), lambda l, j: (l, 0, j)),
            pl.BlockSpec((1, 1, tn), lambda l, j: (l, 0, j)),
        ],
        out_specs=pl.BlockSpec((1, 8, tn), lambda l, j: (l, 0, j)),
        out_shape=jax.ShapeDtypeStruct((depth, 8, n3), F32),
        compiler_params=_params(("parallel", "parallel")),
        name="mod_vectors",
    )(cs, w_mod, b_mod.reshape(depth, 1, n3))


def _proj_kernel(x_ref, g_ref, sc_ref, sh_ref, w_ref, o_ref, h_ref, *, ncol):
    @pl.when(pl.program_id(1) == 0)
    def _():
        x = x_ref[...]
        ms = jnp.mean(x * x, axis=-1, keepdims=True)
        y = x * lax.rsqrt(ms + EPS) * g_ref[...]
        h_ref[...] = (y * (1.0 + sc_ref[0]) + sh_ref[0]).astype(BF16)

    acc = _dot(h_ref[...], w_ref[0].astype(BF16))
    for c in range(ncol):
        o_ref[c] = acc[:, c * LANE:(c + 1) * LANE].astype(BF16)


def _proj_call(x2d, g, scale, shift, w, layer, rows_per_batch, name):
    r, d = x2d.shape
    n = w.shape[2]
    tm = min(PROJ_TM, rows_per_batch)
    tn = PROJ_TN if n % PROJ_TN == 0 else n
    assert r % tm == 0 and rows_per_batch % tm == 0 and n % tn == 0 and tn % LANE == 0
    ncol = tn // LANE
    bidx = lambda i, j: ((i * tm) // rows_per_batch, 0, 0)
    return pl.pallas_call(
        functools.partial(_proj_kernel, ncol=ncol),
        grid=(r // tm, n // tn),
        in_specs=[
            pl.BlockSpec((tm, d), lambda i, j: (i, 0)),
            pl.BlockSpec((1, d), lambda i, j: (0, 0)),
            pl.BlockSpec((1, 1, d), bidx),
            pl.BlockSpec((1, 1, d), bidx),
            pl.BlockSpec((1, d, tn), lambda i, j: (layer, 0, j)),
        ],
        out_specs=pl.BlockSpec((ncol, tm, LANE), lambda i, j: (j, i, 0)),
        out_shape=jax.ShapeDtypeStruct((n // LANE, r, LANE), BF16),
        scratch_shapes=[pltpu.VMEM((tm, d), BF16)],
        compiler_params=_params(("parallel", "arbitrary")),
        name=name,
    )(x2d, g.reshape(1, d), scale, shift, w)


def _out_kernel(u_ref, w_ref, x_ref, gate_ref, gp_ref, o_ref):
    y = _dot(u_ref[...], w_ref[0])
    ms = jnp.mean(y * y, axis=-1, keepdims=True)
    r = y * lax.rsqrt(ms + EPS) * gp_ref[...]
    o_ref[...] = x_ref[...] + gate_ref[0] * r


def _out_call(u, w, layer, x2d, gate, g_post, rows_per_batch, name):
    r, kdim = u.shape
    d = w.shape[2]
    tm = min(OUT_TM, rows_per_batch)
    assert r % tm == 0 and rows_per_batch % tm == 0
    return pl.pallas_call(
        _out_kernel,
        grid=(r // tm,),
        in_specs=[
            pl.BlockSpec((tm, kdim), lambda i: (i, 0)),
            pl.BlockSpec((1, kdim, d), lambda i: (layer, 0, 0), pipeline_mode=pl.Buffered(1)),
            pl.BlockSpec((tm, d), lambda i: (i, 0)),
            pl.BlockSpec((1, 1, d), lambda i: ((i * tm) // rows_per_batch, 0, 0)),
            pl.BlockSpec((1, d), lambda i: (0, 0)),
        ],
        out_specs=pl.BlockSpec((tm, d), lambda i: (i, 0)),
        out_shape=jax.ShapeDtypeStruct((r, d), F32),
        compiler_params=_params(("parallel",)),
        name=name,
    )(u, w, x2d, gate, g_post.reshape(1, d))


def _na_block_plan(rows):
    plans = []
    for qs, bs in ((0, 0), (NA_QROWS, 0), (rows - NA_QROWS, rows - NA_BAND)):
        rq = qs + np.arange(NA_QROWS)
        rk = bs + np.arange(NA_BAND)
        r0 = np.clip(rq - WIN_H // 2, 0, rows - WIN_H)
        row_in = (rk[None, :] >= r0[:, None]) & (rk[None, :] < r0[:, None] + WIN_H)
        plans.append(np.where(row_in, rk[None, :] - rq[:, None] + (WIN_H - 1), -1))
    return plans


def _na_build_bias(rpb_ref, hh, tb_scr, bias_scr, rows):
    nrel_w = 2 * WIN_W - 1
    lane = lax.broadcasted_iota(jnp.int32, (GRID_W, LANE), 1)
    cq = lax.broadcasted_iota(jnp.int32, (GRID_W, LANE), 0)
    ck = lane & (GRID_W - 1)
    didx = ck - cq + (WIN_W - 1)
    c0 = jnp.clip(cq - WIN_W // 2, 0, GRID_W - WIN_W)
    col_in = (ck >= c0) & (ck < c0 + WIN_W)
    for dr in range(2 * WIN_H - 1):
        tile = jnp.zeros((GRID_W, LANE), F32)
        for j in range(nrel_w):
            tile = jnp.where(didx == j, rpb_ref[0, hh, dr * nrel_w + j] * LOG2E, tile)
        tb_scr[hh, dr] = jnp.where(col_in, tile, NEG)
    neg = jnp.full((GRID_W, LANE), NEG, F32)
    for var, plan in enumerate(_na_block_plan(rows)):
        for rq in range(NA_QROWS):
            for pair in range((NA_BAND + 1) // 2):
                rks = [rk for rk in (2 * pair, 2 * pair + 1) if rk < NA_BAND]
                src = [tb_scr[hh, int(plan[rq, rk])] if plan[rq, rk] >= 0 else neg for rk in rks]
                tile = src[0] if len(src) == 1 else jnp.where(lane < GRID_W, src[0], src[1])
                width = GRID_W * len(rks)
                bias_scr[hh, var, rq * GRID_W:(rq + 1) * GRID_W,
                         pair * LANE:pair * LANE + width] = tile[:, :width]


def _na_kernel(rpb_ref, q_ref, k_ref, v_ref, z_ref, kc_ref, vc_ref, o_ref, tb_scr, bias_scr,
               *, rows, heads):
    qb = pl.program_id(2)
    nqb = rows // NA_QROWS

    @pl.when((pl.program_id(1) == 0) & (qb == 0))
    def _():
        for hh in range(heads):
            _na_build_bias(rpb_ref, hh, tb_scr, bias_scr, rows)

    var = jnp.where(qb == 0, 0, jnp.where(qb == nqb - 1, 2, 1))
    rs = jnp.clip(qb * NA_QROWS - WIN_H // 2, 0, rows - NA_BAND)
    start = pl.multiple_of(rs * GRID_W, GRID_W)
    nkeys = NA_BAND * GRID_W
    for hh in range(heads):
        q = (q_ref[hh].astype(F32) * (LANE ** -0.5 * LOG2E)).astype(BF16)
        kb = k_ref[hh, pl.ds(start, nkeys), :]
        vb = v_ref[hh, pl.ds(start, nkeys), :]
        s1 = _nt_dot(q, kb) + bias_scr[hh, var]
        s2 = _nt_dot(q, kc_ref[hh])
        mx = jnp.maximum(jnp.max(s1, axis=-1, keepdims=True), jnp.max(s2, axis=-1, keepdims=True))
        p1 = jnp.exp2(s1 - mx)
        p2 = jnp.exp2(s2 - mx)
        l = jnp.sum(p1, axis=-1, keepdims=True) + jnp.sum(p2, axis=-1, keepdims=True)
        o = (_dot(p1.astype(BF16), vb) + _dot(p2.astype(BF16), vc_ref[hh])) / l
        o_ref[:, hh * LANE:(hh + 1) * LANE] = (o * _silu(z_ref[hh].astype(F32))).astype(BF16)


def _na_call(p, pc, rpb, nb, t, m, nh):
    rows = t // GRID_W
    tq = NA_QROWS * GRID_W
    nqb = t // tq
    nq, nk = tq, NA_BAND * GRID_W
    nrel = rpb.shape[1] * rpb.shape[2]
    hb = NA_HEADS if nh % NA_HEADS == 0 else 1
    ng = nh // hb
    return pl.pallas_call(
        functools.partial(_na_kernel, rows=rows, heads=hb),
        grid=(ng, nb, nqb),
        in_specs=[
            pl.BlockSpec((1, hb, nrel), lambda h, b, i: (h, 0, 0), memory_space=pltpu.SMEM),
            pl.BlockSpec((hb, tq, LANE), lambda h, b, i: (h, b * nqb + i, 0)),
            pl.BlockSpec((hb, t, LANE), lambda h, b, i: (ng + h, b, 0)),
            pl.BlockSpec((hb, t, LANE), lambda h, b, i: (2 * ng + h, b, 0)),
            pl.BlockSpec((hb, tq, LANE), lambda h, b, i: (3 * ng + h, b * nqb + i, 0)),
            pl.BlockSpec((hb, m, LANE), lambda h, b, i: (ng + h, b, 0)),
            pl.BlockSpec((hb, m, LANE), lambda h, b, i: (2 * ng + h, b, 0)),
        ],
        out_specs=pl.BlockSpec((tq, hb * LANE), lambda h, b, i: (b * nqb + i, h)),
        out_shape=jax.ShapeDtypeStruct((nb * t, nh * LANE), BF16),
        scratch_shapes=[
            pltpu.VMEM((hb, 2 * WIN_H - 1, GRID_W, LANE), F32),
            pltpu.VMEM((hb, 3, nq, nk), F32),
        ],
        compiler_params=_params(("parallel", "arbitrary", "arbitrary")),
        name="na_attention",
    )(rpb.reshape(ng, hb, nrel), p, p, p, p, pc, pc)


def _ctx_attn_kernel(q_ref, k_ref, v_ref, z_ref, o_ref):
    q = (q_ref[0].astype(F32) * (LANE ** -0.5)).astype(BF16)
    s = _nt_dot(q, k_ref[0])
    m = jnp.max(s, axis=-1, keepdims=True)
    p = jnp.exp(s - m)
    l = jnp.sum(p, axis=-1, keepdims=True)
    o = _dot(p.astype(BF16), v_ref[0]) / l
    o_ref[...] = (o * _silu(z_ref[0].astype(F32))).astype(BF16)


def _ctx_attn_call(pc, nb, m, nh):
    return pl.pallas_call(
        _ctx_attn_kernel,
        grid=(nh, nb),
        in_specs=[
            pl.BlockSpec((1, m, LANE), lambda h, b: (h, b, 0)),
            pl.BlockSpec((1, m, LANE), lambda h, b: (nh + h, b, 0)),
            pl.BlockSpec((1, m, LANE), lambda h, b: (2 * nh + h, b, 0)),
            pl.BlockSpec((1, m, LANE), lambda h, b: (3 * nh + h, b, 0)),
        ],
        out_specs=pl.BlockSpec((m, LANE), lambda h, b: (b, h)),
        out_shape=jax.ShapeDtypeStruct((nb * m, nh * LANE), BF16),
        compiler_params=_params(("parallel", "parallel")),
        name="ctx_attention",
    )(pc, pc, pc, pc)


def _pool_kernel(u_ref, w_ref, z_ref, s_ref, o_ref, lhs_scr, *, n, kcol, ncol):
    group = pl.program_id(0)
    t = lax.broadcasted_iota(jnp.int32, (n, LANE), 0)

    def up(a, k):
        return jnp.where(t < n - k, pltpu.roll(a, n - k, axis=0), 0.0)

    def down(a, k):
        return jnp.where(t >= k, pltpu.roll(a, k, axis=0), 0.0)

    for gi, w in enumerate(POOL_SIZES):
        half = w // 2

        @pl.when((group == gi) & (pl.program_id(2) == 0))
        def _(half=half):
            cnt = (jnp.minimum(t + half, n) - jnp.maximum(t - half, 0)).astype(F32)
            for c in range(kcol):
                x = u_ref[c].astype(F32)
                fwd, bwd, k = x, x, 1
                while k < half:
                    fwd = fwd + up(fwd, k)
                    bwd = bwd + down(bwd, k)
                    k *= 2
                total = fwd + down(bwd, 1)
                lhs_scr[:, c * LANE:(c + 1) * LANE] = (total / cnt - x).astype(BF16)

    acc = _dot(lhs_scr[...], w_ref[0])
    z = jnp.concatenate([z_ref[c] for c in range(ncol)], axis=-1).astype(F32)
    o_ref[...] = (acc * s_ref[...] * _silu(z)).astype(BF16)


def _pool_call(p, w_grp, scale, nb, n):
    ng, cg, _ = w_grp.shape
    assert ng == len(POOL_SIZES)
    tn = min(512, cg)
    kcol, ncol, nj = cg // LANE, tn // LANE, cg // tn
    zoff = (ng * cg) // tn
    return pl.pallas_call(
        functools.partial(_pool_kernel, n=n, kcol=kcol, ncol=ncol),
        grid=(ng, nb, nj),
        in_specs=[
            pl.BlockSpec((kcol, n, LANE), lambda g, b, j: (g, b, 0)),
            pl.BlockSpec((1, cg, tn), lambda g, b, j: (g, 0, j)),
            pl.BlockSpec((ncol, n, LANE), lambda g, b, j: (zoff + g * nj + j, b, 0)),
            pl.BlockSpec((1, tn), lambda g, b, j: (0, g * nj + j)),
        ],
        out_specs=pl.BlockSpec((n, tn), lambda g, b, j: (b, g * nj + j)),
        out_shape=jax.ShapeDtypeStruct((nb * n, ng * cg), BF16),
        scratch_shapes=[pltpu.VMEM((n, cg), BF16)],
        compiler_params=_params(("parallel", "parallel", "arbitrary")),
        name="pool_mixer",
    )(p, w_grp, p, scale.reshape(1, ng * cg))


HGRN_LEVELS = (128, 64, 32, 16)
HGRN_DIAG = 8


def _hgrn_tri():
    t = np.arange(SLAB)
    fwd = (t[None, :] <= t[:, None]).astype(np.float32)
    tri = np.stack([np.concatenate([fwd, fwd], axis=1), np.concatenate([fwd.T, fwd.T], axis=1)])
    return jnp.asarray(tri, BF16)


def _hgrn_kernel(q_ref, af_ref, ab_ref, v_ref, z_ref, qc_ref, afc_ref, abc_ref, vc_ref, zc_ref,
                 lb_ref, gn_ref, tri_ref, u_ref, uc_ref,
                 o_scr, qh_scr, kb_scr, vt_scr, g_scr, qs_scr, kk_scr, b_scr, v_scr, ds_scr, sin_scr,
                 *, layer, t, m):
    nslab_c = m // SLAB
    nslab = (m + t) // SLAB

    lbs = []
    for d in (0, 1):
        raw = lb_ref[d]
        e = jnp.exp(raw - jnp.max(raw, axis=0, keepdims=True))
        probs = e / jnp.sum(e, axis=0, keepdims=True)
        lbs.append(jnp.sum(probs[1:layer + 1], axis=0, keepdims=True) if layer >= 1
                   else jnp.zeros((1, LANE), F32))

    ti = lax.broadcasted_iota(jnp.int32, (SLAB, SLAB), 0)
    si = lax.broadcasted_iota(jnp.int32, (SLAB, SLAB), 1)
    rt = lax.broadcasted_iota(jnp.int32, (SLAB, LANE), 0)
    diag_masks, level_masks = [], []
    for d in (0, 1):
        before = (si < ti) if d == 0 else (si > ti)
        diag_masks.append(((ti // HGRN_DIAG) == (si // HGRN_DIAG)) & (before | (si == ti)))
        level_masks.append({blk: ((ti // blk) == (si // blk)) & ((ti // (blk // 2)) != (si // (blk // 2)))
                            & before for blk in HGRN_LEVELS})

    def ref_delta(b, block, ref_row):
        b3 = b.reshape(SLAB // block, block, LANE)
        return (b3 - b3[:, ref_row:ref_row + 1, :]).reshape(SLAB, LANE)

    def gates_slab(srcs, r0, slab):
        row = slab * SLAB
        qs = _silu(srcs[0][0, pl.ds(r0, SLAB), :].astype(F32))
        qs_scr[pl.ds(row, SLAB), :] = qs
        v = srcs[3][0, pl.ds(r0, SLAB), :]
        v_scr[pl.ds(row, SLAB), :] = v
        vt_scr[slab] = v.astype(F32).T.astype(BF16)
        for d in (0, 1):
            a = srcs[1 + d][0, pl.ds(r0, SLAB), :].astype(F32)
            f = lbs[d] + (1.0 - lbs[d]) * (1.0 / (1.0 + jnp.exp(-a)))
            lf = jnp.log2(f)
            kk = 1.0 - f
            hi = lf.astype(BF16)
            lo = (lf - hi.astype(F32)).astype(BF16)
            b = _dot(tri_ref[d], jnp.concatenate([hi, lo], axis=0))
            b_end = b[SLAB - 1:SLAB] if d == 0 else b[0:1]
            kk_scr[d, pl.ds(row, SLAB), :] = kk
            b_scr[d, pl.ds(row, SLAB), :] = b
            qh_scr[d, pl.ds(row, SLAB), :] = (qs * jnp.exp2(b)).astype(BF16)
            kb_scr[d, pl.ds(row, SLAB), :] = (kk * jnp.exp2(b_end - b)).astype(BF16)
            g_scr[d, pl.ds(slab, 1), :] = jnp.exp2(b_end)

    def mix_slab(slab):
        row = pl.multiple_of(slab * SLAB, SLAB)
        qs = qs_scr[pl.ds(row, SLAB), :]
        v = v_scr[pl.ds(row, SLAB), :]
        atts = []
        for d in (0, 1):
            kk = kk_scr[d, pl.ds(row, SLAB), :]
            b = b_scr[d, pl.ds(row, SLAB), :]
            dlt = ref_delta(b, HGRN_DIAG, HGRN_DIAG // 2 - 1 if d == 0 else HGRN_DIAG // 2)
            qt = (qs * jnp.exp2(dlt)).astype(BF16)
            kt = (kk * jnp.exp2(-dlt)).astype(BF16)
            att = jnp.where(diag_masks[d], _nt_dot(qt, kt), 0.0)
            for block in HGRN_LEVELS:
                half = block // 2
                dlt = ref_delta(b, block, half - 1 if d == 0 else half)
                pos = rt & (block - 1)
                is_target = (pos >= half) if d == 0 else (pos < half)
                x = (jnp.where(is_target, qs, kk) * jnp.exp2(-jnp.abs(dlt))).astype(BF16)
                att = jnp.where(level_masks[d][block], _nt_dot(x, x), att)
            atts.append(att.astype(BF16))
        o = _dot(jnp.concatenate(atts, axis=1), jnp.concatenate([v, v], axis=0))
        for d in (0, 1):
            o = o + _nt_dot(qh_scr[d, pl.ds(row, SLAB), :], sin_scr[d, slab])
        o_scr[pl.ds(row, SLAB), :] = o

    csrc = (qc_ref, afc_ref, abc_ref, vc_ref)
    lsrc = (q_ref, af_ref, ab_ref, v_ref)
    for s in range(nslab_c):
        gates_slab(csrc, s * SLAB, s)

    def gates_body(s, carry):
        gates_slab(lsrc, pl.multiple_of(s * SLAB, SLAB), s + nslab_c)
        return carry

    lax.fori_loop(0, t // SLAB, gates_body, 0, unroll=2)

    def ds_body(s, carry):
        row = pl.multiple_of(s * SLAB, SLAB)
        for d in (0, 1):
            ds_scr[d, s] = _dot(vt_scr[s], kb_scr[d, pl.ds(row, SLAB), :])
        return carry

    lax.fori_loop(0, nslab, ds_body, 0, unroll=3)

    orders = (list(range(nslab)),
              list(range(nslab_c - 1, -1, -1)) + list(range(nslab - 1, nslab_c - 1, -1)))
    for d in (0, 1):
        st = jnp.zeros((LANE, LANE), F32)
        for slab in orders[d]:
            sin_scr[d, slab] = st.astype(BF16)
            st = g_scr[d, slab:slab + 1, :] * st + ds_scr[d, slab]

    def mix_body(s, carry):
        mix_slab(s)
        return carry

    lax.fori_loop(0, nslab, mix_body, 0, unroll=6)

    def finish(r0, n, z_blk, out_ref):
        o = o_scr[r0:r0 + n, :]
        ms = jnp.mean(o * o, axis=-1, keepdims=True)
        on = o * lax.rsqrt(ms + EPS) * gn_ref[...]
        out_ref[...] = (on * _silu(z_blk.astype(F32))).astype(BF16)

    finish(0, m, zc_ref[0], uc_ref)
    finish(m, t, z_ref[0], u_ref)


def _hgrn_call(p, pc, lb_raw, g_norm, layer, nb, t, m, nh):
    depth = lb_raw.shape[1]
    tt = m + t
    assert t % (2 * SLAB) == 0 and m % SLAB == 0 and (tt // SLAB) % 6 == 0
    lat = lambda k: pl.BlockSpec((1, t, LANE), lambda b, h, k=k: (k * nh + h, b, 0))
    ctx = lambda k: pl.BlockSpec((1, m, LANE), lambda b, h, k=k: (k * nh + h, b, 0))
    return pl.pallas_call(
        functools.partial(_hgrn_kernel, layer=layer, t=t, m=m),
        grid=(nb, nh),
        in_specs=[lat(k) for k in range(5)] + [ctx(k) for k in range(5)] + [
            pl.BlockSpec((2, depth, LANE), lambda b, h: (0, 0, h)),
            pl.BlockSpec((1, LANE), lambda b, h: (0, h)),
            pl.BlockSpec((2, SLAB, 2 * SLAB), lambda b, h: (0, 0, 0)),
        ],
        out_specs=[
            pl.BlockSpec((t, LANE), lambda b, h: (b, h)),
            pl.BlockSpec((m, LANE), lambda b, h: (b, h)),
        ],
        out_shape=[
            jax.ShapeDtypeStruct((nb * t, nh * LANE), BF16),
            jax.ShapeDtypeStruct((nb * m, nh * LANE), BF16),
        ],
        scratch_shapes=[
            pltpu.VMEM((tt, LANE), F32),
            pltpu.VMEM((2, tt, LANE), BF16),
            pltpu.VMEM((2, tt, LANE), BF16),
            pltpu.VMEM((tt // SLAB, LANE, SLAB), BF16),
            pltpu.VMEM((2, tt // SLAB, LANE), F32),
            pltpu.VMEM((tt, LANE), F32),
            pltpu.VMEM((2, tt, LANE), F32),
            pltpu.VMEM((2, tt, LANE), F32),
            pltpu.VMEM((tt, LANE), BF16),
            pltpu.VMEM((2, tt // SLAB, LANE, LANE), F32),
            pltpu.VMEM((2, tt // SLAB, LANE, LANE), BF16),
        ],
        compiler_params=_params(("parallel", "parallel")),
        name="hgrn_scan",
    )(p, p, p, p, p, pc, pc, pc, pc, pc, lb_raw, g_norm.reshape(1, nh * LANE), _hgrn_tri())


def kernel(x, c, ctx, c_ctx, w_mod, b_mod, g_pre, g_post, na_w_in, na_rpb, na_w_out,
           pool_w_in, pool_w_grp, pool_scale, pool_w_out, hgrn_w_in, hgrn_lb, hgrn_gnorm, hgrn_w_out):
    nb, t, d = x.shape
    m = ctx.shape[1]
    depth = w_mod.shape[0]
    nh = na_w_out.shape[1] // LANE
    rows = t // GRID_W

    mods = _mod_call(c, c_ctx, w_mod, b_mod)
    w_ins = (na_w_in, pool_w_in, hgrn_w_in)
    w_outs = tuple(w.astype(BF16) for w in (na_w_out, pool_w_out, hgrn_w_out))
    xl = x.reshape(nb * t, d)
    xc = ctx.reshape(nb * m, d)

    for i in range(depth):
        kind, j = i % 3, i // 3
        need_ctx = i < depth - 1
        mod = mods[i]
        shift, scale, gate = (mod[:nb, None, k * d:(k + 1) * d] for k in range(3))
        shift_c, scale_c, gate_c = (mod[nb:nb + 1, None, k * d:(k + 1) * d] for k in range(3))
        w_in, w_out = w_ins[kind], w_outs[kind]

        p = _proj_call(xl, g_pre[i], scale, shift, w_in, j, t, f"proj_lat_{i}")
        pc = _proj_call(xc, g_pre[i], scale_c, shift_c, w_in, j, nb * m, f"proj_ctx_{i}")

        uc = None
        if kind == 0:
            u = _na_call(p, pc, na_rpb[j], nb, t, m, nh)
            if need_ctx:
                uc = _ctx_attn_call(pc, nb, m, nh)
        elif kind == 1:
            wg = pool_w_grp[j].astype(BF16)
            u = _pool_call(p, wg, pool_scale[j], nb, t)
            if need_ctx:
                uc = _pool_call(pc, wg, pool_scale[j], nb, m)
        else:
            u, uc = _hgrn_call(p, pc, hgrn_lb, hgrn_gnorm[j], i, nb, t, m, nh)

        xl = _out_call(u, w_out, j, xl, gate, g_post[i], t, f"out_lat_{i}")
        if need_ctx:
            xc = _out_call(uc, w_out, j, xc, gate_c, g_post[i], nb * m, f"out_ctx_{i}")

    return xl.reshape(nb, t, d)
```

```python
import functools

import numpy as np
import jax
import jax.numpy as jnp
from jax import lax
from jax.experimental import pallas as pl
from jax.experimental.pallas import tpu as pltpu

F32 = jnp.float32
BF16 = jnp.bfloat16

LANE = 128
VMEM_LIMIT = 56 * 1024 * 1024

EPS = 1e-6
GRID_W = 64
WIN_H = 8
WIN_W = 16
POOL_SIZES = (2, 4, 8, 16)
NEG = -1e30

NA_QROWS = 4
NA_HEADS = 4
LOG2E = 1.4426950408889634
NA_BAND = NA_QROWS + WIN_H - 1
SLAB = 128
PROJ_TM, PROJ_TN = 1024, 1024
OUT_TM = 512


def _silu(x):
    return x / (1.0 + jnp.exp(-x))


def _nt_dot(a, b):
    return lax.dot_general(a, b, (((1,), (1,)), ((), ())), preferred_element_type=F32)


def _dot(a, b):
    return jnp.dot(a, b, preferred_element_type=F32)


def _params(sem, vmem=VMEM_LIMIT):
    return pltpu.CompilerParams(dimension_semantics=sem, vmem_limit_bytes=vmem)


def _mod_kernel(c_ref, w_ref, b_ref, o_ref):
    s = _silu(c_ref[...])
    s_hi = s.astype(BF16)
    s_lo = (s - s_hi.astype(F32)).astype(BF16)
    w = w_ref[0]
    w_hi = w.astype(BF16)
    w_lo = (w - w_hi.astype(F32)).astype(BF16)
    acc = _dot(s_hi, w_hi) + _dot(s_lo, w_hi) + _dot(s_hi, w_lo)
    o_ref[0] = acc + b_ref[0]


def _mod_call(c, c_ctx, w_mod, b_mod):
    depth, d, n3 = w_mod.shape
    nb = c.shape[0]
    assert nb + 1 <= 8
    cs = jnp.zeros((8, d), F32).at[:nb].set(c).at[nb].set(c_ctx)
    tn = 1024 if n3 % 1024 == 0 else n3
    return pl.pallas_call(
        _mod_kernel,
        grid=(depth, n3 // tn),
        in_specs=[
            pl.BlockSpec((8, d), lambda l, j: (0, 0)),
            pl.BlockSpec((1, d, tn), lambda l, j: (l, 0, j)),
            pl.BlockSpec((1, 1, tn), lambda l, j: (l, 0, j)),
        ],
        out_specs=pl.BlockSpec((1, 8, tn), lambda l, j: (l, 0, j)),
        out_shape=jax.ShapeDtypeStruct((depth, 8, n3), F32),
        compiler_params=_params(("parallel", "parallel")),
        name="mod_vectors",
    )(cs, w_mod, b_mod.reshape(depth, 1, n3))


def _proj_kernel(x_ref, g_ref, sc_ref, sh_ref, w_ref, o_ref, h_ref, *, ncol):
    @pl.when(pl.program_id(1) == 0)
    def _():
        x = x_ref[...]
        ms = jnp.mean(x * x, axis=-1, keepdims=True)
        y = x * lax.rsqrt(ms + EPS) * g_ref[...]
        h_ref[...] = (y * (1.0 + sc_ref[0]) + sh_ref[0]).astype(BF16)

    acc = _dot(h_ref[...], w_ref[0].astype(BF16))
    for c in range(ncol):
        o_ref[c] = acc[:, c * LANE:(c + 1) * LANE].astype(BF16)


def _proj_call(x2d, g, scale, shift, w, layer, rows_per_batch, name):
    r, d = x2d.shape
    n = w.shape[2]
    tm = min(PROJ_TM, rows_per_batch)
    tn = PROJ_TN if n % PROJ_TN == 0 else n
    assert r % tm == 0 and rows_per_batch % tm == 0 and n % tn == 0 and tn % LANE == 0
    ncol = tn // LANE
    bidx = lambda i, j: ((i * tm) // rows_per_batch, 0, 0)
    return pl.pallas_call(
        functools.partial(_proj_kernel, ncol=ncol),
        grid=(r // tm, n // tn),
        in_specs=[
            pl.BlockSpec((tm, d), lambda i, j: (i, 0)),
            pl.BlockSpec((1, d), lambda i, j: (0, 0)),
            pl.BlockSpec((1, 1, d), bidx),
            pl.BlockSpec((1, 1, d), bidx),
            pl.BlockSpec((1, d, tn), lambda i, j: (layer, 0, j)),
        ],
        out_specs=pl.BlockSpec((ncol, tm, LANE), lambda i, j: (j, i, 0)),
        out_shape=jax.ShapeDtypeStruct((n // LANE, r, LANE), BF16),
        scratch_shapes=[pltpu.VMEM((tm, d), BF16)],
        compiler_params=_params(("parallel", "arbitrary")),
        name=name,
    )(x2d, g.reshape(1, d), scale, shift, w)


def _out_kernel(u_ref, w_ref, x_ref, gate_ref, gp_ref, o_ref):
    y = _dot(u_ref[...], w_ref[0])
    ms = jnp.mean(y * y, axis=-1, keepdims=True)
    r = y * lax.rsqrt(ms + EPS) * gp_ref[...]
    o_ref[...] = x_ref[...] + gate_ref[0] * r


def _out_call(u, w, layer, x2d, gate, g_post, rows_per_batch, name):
    r, kdim = u.shape
    d = w.shape[2]
    tm = min(OUT_TM, rows_per_batch)
    assert r % tm == 0 and rows_per_batch % tm == 0
    return pl.pallas_call(
        _out_kernel,
        grid=(r // tm,),
        in_specs=[
            pl.BlockSpec((tm, kdim), lambda i: (i, 0)),
            pl.BlockSpec((1, kdim, d), lambda i: (layer, 0, 0), pipeline_mode=pl.Buffered(1)),
            pl.BlockSpec((tm, d), lambda i: (i, 0)),
            pl.BlockSpec((1, 1, d), lambda i: ((i * tm) // rows_per_batch, 0, 0)),
            pl.BlockSpec((1, d), lambda i: (0, 0)),
        ],
        out_specs=pl.BlockSpec((tm, d), lambda i: (i, 0)),
        out_shape=jax.ShapeDtypeStruct((r, d), F32),
        compiler_params=_params(("parallel",)),
        name=name,
    )(u, w, x2d, gate, g_post.reshape(1, d))


def _na_block_plan(rows):
    plans = []
    for qs, bs in ((0, 0), (NA_QROWS, 0), (rows - NA_QROWS, rows - NA_BAND)):
        rq = qs + np.arange(NA_QROWS)
        rk = bs + np.arange(NA_BAND)
        r0 = np.clip(rq - WIN_H // 2, 0, rows - WIN_H)
        row_in = (rk[None, :] >= r0[:, None]) & (rk[None, :] < r0[:, None] + WIN_H)
        plans.append(np.where(row_in, rk[None, :] - rq[:, None] + (WIN_H - 1), -1))
    return plans


def _na_build_bias(rpb_ref, hh, tb_scr, bias_scr, rows):
    nrel_w = 2 * WIN_W - 1
    lane = lax.broadcasted_iota(jnp.int32, (GRID_W, LANE), 1)
    cq = lax.broadcasted_iota(jnp.int32, (GRID_W, LANE), 0)
    ck = lane & (GRID_W - 1)
    didx = ck - cq + (WIN_W - 1)
    c0 = jnp.clip(cq - WIN_W // 2, 0, GRID_W - WIN_W)
    col_in = (ck >= c0) & (ck < c0 + WIN_W)
    for dr in range(2 * WIN_H - 1):
        tile = jnp.zeros((GRID_W, LANE), F32)
        for j in range(nrel_w):
            tile = jnp.where(didx == j, rpb_ref[0, hh, dr * nrel_w + j] * LOG2E, tile)
        tb_scr[hh, dr] = jnp.where(col_in, tile, NEG)
    neg = jnp.full((GRID_W, LANE), NEG, F32)
    for var, plan in enumerate(_na_block_plan(rows)):
        for rq in range(NA_QROWS):
            for pair in range((NA_BAND + 1) // 2):
                rks = [rk for rk in (2 * pair, 2 * pair + 1) if rk < NA_BAND]
                src = [tb_scr[hh, int(plan[rq, rk])] if plan[rq, rk] >= 0 else neg for rk in rks]
                tile = src[0] if len(src) == 1 else jnp.where(lane < GRID_W, src[0], src[1])
                width = GRID_W * len(rks)
                bias_scr[hh, var, rq * GRID_W:(rq + 1) * GRID_W,
                         pair * LANE:pair * LANE + width] = tile[:, :width]


def _na_kernel(rpb_ref, q_ref, k_ref, v_ref, z_ref, kc_ref, vc_ref, o_ref, tb_scr, bias_scr,
               *, rows, heads):
    nqb = rows // NA_QROWS
    tq = NA_QROWS * GRID_W
    nkeys = NA_BAND * GRID_W

    @pl.when(pl.program_id(1) == 0)
    def _():
        for hh in range(heads):
            _na_build_bias(rpb_ref, hh, tb_scr, bias_scr, rows)

    def query_block(qb, carry):
        var = jnp.where(qb == 0, 0, jnp.where(qb == nqb - 1, 2, 1))
        rs = jnp.clip(qb * NA_QROWS - WIN_H // 2, 0, rows - NA_BAND)
        start = pl.multiple_of(rs * GRID_W, GRID_W)
        row = pl.multiple_of(qb * tq, tq)
        for hh in range(heads):
            q = (q_ref[hh, pl.ds(row, tq), :].astype(F32) * (LANE ** -0.5 * LOG2E)).astype(BF16)
            kb = k_ref[hh, pl.ds(start, nkeys), :]
            vb = v_ref[hh, pl.ds(start, nkeys), :]
            s1 = _nt_dot(q, kb) + bias_scr[hh, var]
            s2 = _nt_dot(q, kc_ref[hh])
            mx = jnp.maximum(jnp.max(s1, axis=-1, keepdims=True), jnp.max(s2, axis=-1, keepdims=True))
            p1 = jnp.exp2(s1 - mx)
            p2 = jnp.exp2(s2 - mx)
            l = jnp.sum(p1, axis=-1, keepdims=True) + jnp.sum(p2, axis=-1, keepdims=True)
            o = (_dot(p1.astype(BF16), vb) + _dot(p2.astype(BF16), vc_ref[hh])) / l
            z = z_ref[hh, pl.ds(row, tq), :].astype(F32)
            o_ref[pl.ds(row, tq), hh * LANE:(hh + 1) * LANE] = (o * _silu(z)).astype(BF16)
        return carry

    lax.fori_loop(0, nqb, query_block, 0)


def _na_call(p, pc, rpb, nb, t, m, nh):
    rows = t // GRID_W
    nq, nk = NA_QROWS * GRID_W, NA_BAND * GRID_W
    nrel = rpb.shape[1] * rpb.shape[2]
    hb = NA_HEADS if nh % NA_HEADS == 0 else 1
    ng = nh // hb
    lat = lambda k: pl.BlockSpec((hb, t, LANE), lambda h, b, k=k: (k * ng + h, b, 0))
    ctx = lambda k: pl.BlockSpec((hb, m, LANE), lambda h, b, k=k: (k * ng + h, b, 0))
    return pl.pallas_call(
        functools.partial(_na_kernel, rows=rows, heads=hb),
        grid=(ng, nb),
        in_specs=[
            pl.BlockSpec((1, hb, nrel), lambda h, b: (h, 0, 0), memory_space=pltpu.SMEM),
            lat(0), lat(1), lat(2), lat(3), ctx(1), ctx(2),
        ],
        out_specs=pl.BlockSpec((t, hb * LANE), lambda h, b: (b, h)),
        out_shape=jax.ShapeDtypeStruct((nb * t, nh * LANE), BF16),
        scratch_shapes=[
            pltpu.VMEM((hb, 2 * WIN_H - 1, GRID_W, LANE), F32),
            pltpu.VMEM((hb, 3, nq, nk), F32),
        ],
        compiler_params=_params(("parallel", "arbitrary")),
        name="na_attention",
    )(rpb.reshape(ng, hb, nrel), p, p, p, p, pc, pc)


def _ctx_attn_kernel(q_ref, k_ref, v_ref, z_ref, o_ref):
    q = (q_ref[0].astype(F32) * (LANE ** -0.5)).astype(BF16)
    s = _nt_dot(q, k_ref[0])
    m = jnp.max(s, axis=-1, keepdims=True)
    p = jnp.exp(s - m)
    l = jnp.sum(p, axis=-1, keepdims=True)
    o = _dot(p.astype(BF16), v_ref[0]) / l
    o_ref[...] = (o * _silu(z_ref[0].astype(F32))).astype(BF16)


def _ctx_attn_call(pc, nb, m, nh):
    return pl.pallas_call(
        _ctx_attn_kernel,
        grid=(nh, nb),
        in_specs=[
            pl.BlockSpec((1, m, LANE), lambda h, b: (h, b, 0)),
            pl.BlockSpec((1, m, LANE), lambda h, b: (nh + h, b, 0)),
            pl.BlockSpec((1, m, LANE), lambda h, b: (2 * nh + h, b, 0)),
            pl.BlockSpec((1, m, LANE), lambda h, b: (3 * nh + h, b, 0)),
        ],
        out_specs=pl.BlockSpec((m, LANE), lambda h, b: (b, h)),
        out_shape=jax.ShapeDtypeStruct((nb * m, nh * LANE), BF16),
        compiler_params=_params(("parallel", "parallel")),
        name="ctx_attention",
    )(pc, pc, pc, pc)


def _pool_kernel(u_ref, w_ref, z_ref, s_ref, o_ref, lhs_scr, *, n, kcol, ncol):
    group = pl.program_id(0)
    t = lax.broadcasted_iota(jnp.int32, (n, LANE), 0)

    def up(a, k):
        return jnp.where(t < n - k, pltpu.roll(a, n - k, axis=0), 0.0)

    def down(a, k):
        return jnp.where(t >= k, pltpu.roll(a, k, axis=0), 0.0)

    for gi, w in enumerate(POOL_SIZES):
        half = w // 2

        @pl.when((group == gi) & (pl.program_id(2) == 0))
        def _(half=half):
            cnt = (jnp.minimum(t + half, n) - jnp.maximum(t - half, 0)).astype(F32)
            for c in range(kcol):
                x = u_ref[c].astype(F32)
                fwd, bwd, k = x, x, 1
                while k < half:
                    fwd = fwd + up(fwd, k)
                    bwd = bwd + down(bwd, k)
                    k *= 2
                total = fwd + down(bwd, 1)
                lhs_scr[:, c * LANE:(c + 1) * LANE] = (total / cnt - x).astype(BF16)

    acc = _dot(lhs_scr[...], w_ref[0])
    z = jnp.concatenate([z_ref[c] for c in range(ncol)], axis=-1).astype(F32)
    o_ref[...] = (acc * s_ref[...] * _silu(z)).astype(BF16)


def _pool_call(p, w_grp, scale, nb, n):
    ng, cg, _ = w_grp.shape
    assert ng == len(POOL_SIZES)
    tn = min(512, cg)
    kcol, ncol, nj = cg // LANE, tn // LANE, cg // tn
    zoff = (ng * cg) // tn
    return pl.pallas_call(
        functools.partial(_pool_kernel, n=n, kcol=kcol, ncol=ncol),
        grid=(ng, nb, nj),
        in_specs=[
            pl.BlockSpec((kcol, n, LANE), lambda g, b, j: (g, b, 0)),
            pl.BlockSpec((1, cg, tn), lambda g, b, j: (g, 0, j)),
            pl.BlockSpec((ncol, n, LANE), lambda g, b, j: (zoff + g * nj + j, b, 0)),
            pl.BlockSpec((1, tn), lambda g, b, j: (0, g * nj + j)),
        ],
        out_specs=pl.BlockSpec((n, tn), lambda g, b, j: (b, g * nj + j)),
        out_shape=jax.ShapeDtypeStruct((nb * n, ng * cg), BF16),
        scratch_shapes=[pltpu.VMEM((n, cg), BF16)],
        compiler_params=_params(("parallel", "parallel", "arbitrary")),
        name="pool_mixer",
    )(p, w_grp, p, scale.reshape(1, ng * cg))


HGRN_LEVELS = (128, 64, 32, 16)
HGRN_DIAG = 8


def _hgrn_tri():
    t = np.arange(SLAB)
    fwd = (t[None, :] <= t[:, None]).astype(np.float32)
    tri = np.stack([np.concatenate([fwd, fwd], axis=1), np.concatenate([fwd.T, fwd.T], axis=1)])
    return jnp.asarray(tri, BF16)


def _hgrn_kernel(q_ref, af_ref, ab_ref, v_ref, z_ref, qc_ref, afc_ref, abc_ref, vc_ref, zc_ref,
                 lb_ref, gn_ref, tri_ref, u_ref, uc_ref,
                 o_scr, qh_scr, kb_scr, vt_scr, g_scr, qs_scr, kk_scr, b_scr, v_scr, ds_scr, sin_scr,
                 *, layer, t, m):
    nslab_c = m // SLAB
    nslab = (m + t) // SLAB

    lbs = []
    for d in (0, 1):
        raw = lb_ref[d]
        e = jnp.exp(raw - jnp.max(raw, axis=0, keepdims=True))
        probs = e / jnp.sum(e, axis=0, keepdims=True)
        lbs.append(jnp.sum(probs[1:layer + 1], axis=0, keepdims=True) if layer >= 1
                   else jnp.zeros((1, LANE), F32))

    ti = lax.broadcasted_iota(jnp.int32, (SLAB, SLAB), 0)
    si = lax.broadcasted_iota(jnp.int32, (SLAB, SLAB), 1)
    rt = lax.broadcasted_iota(jnp.int32, (SLAB, LANE), 0)
    diag_masks, level_masks = [], []
    for d in (0, 1):
        before = (si < ti) if d == 0 else (si > ti)
        diag_masks.append(((ti // HGRN_DIAG) == (si // HGRN_DIAG)) & (before | (si == ti)))
        level_masks.append({blk: ((ti // blk) == (si // blk)) & ((ti // (blk // 2)) != (si // (blk // 2)))
                            & before for blk in HGRN_LEVELS})

    def ref_delta(b, block, ref_row):
        b3 = b.reshape(SLAB // block, block, LANE)
        return (b3 - b3[:, ref_row:ref_row + 1, :]).reshape(SLAB, LANE)

    def gates_slab(srcs, r0, slab):
        row = slab * SLAB
        qs = _silu(srcs[0][0, pl.ds(r0, SLAB), :].astype(F32))
        qs_scr[pl.ds(row, SLAB), :] = qs
        v = srcs[3][0, pl.ds(r0, SLAB), :]
        v_scr[pl.ds(row, SLAB), :] = v
        vt_scr[slab] = v.astype(F32).T.astype(BF16)
        for d in (0, 1):
            a = srcs[1 + d][0, pl.ds(r0, SLAB), :].astype(F32)
            f = lbs[d] + (1.0 - lbs[d]) * (1.0 / (1.0 + jnp.exp(-a)))
            lf = jnp.log2(f)
            kk = 1.0 - f
            hi = lf.astype(BF16)
            lo = (lf - hi.astype(F32)).astype(BF16)
            b = _dot(tri_ref[d], jnp.concatenate([hi, lo], axis=0))
            b_end = b[SLAB - 1:SLAB] if d == 0 else b[0:1]
            kk_scr[d, pl.ds(row, SLAB), :] = kk
            b_scr[d, pl.ds(row, SLAB), :] = b
            qh_scr[d, pl.ds(row, SLAB), :] = (qs * jnp.exp2(b)).astype(BF16)
            kb_scr[d, pl.ds(row, SLAB), :] = (kk * jnp.exp2(b_end - b)).astype(BF16)
            g_scr[d, pl.ds(slab, 1), :] = jnp.exp2(b_end)

    def mix_slab(slab):
        row = pl.multiple_of(slab * SLAB, SLAB)
        qs = qs_scr[pl.ds(row, SLAB), :]
        v = v_scr[pl.ds(row, SLAB), :]
        atts = []
        for d in (0, 1):
            kk = kk_scr[d, pl.ds(row, SLAB), :]
            b = b_scr[d, pl.ds(row, SLAB), :]
            dlt = ref_delta(b, HGRN_DIAG, HGRN_DIAG // 2 - 1 if d == 0 else HGRN_DIAG // 2)
            qt = (qs * jnp.exp2(dlt)).astype(BF16)
            kt = (kk * jnp.exp2(-dlt)).astype(BF16)
            att = jnp.where(diag_masks[d], _nt_dot(qt, kt), 0.0)
            for block in HGRN_LEVELS:
                half = block // 2
                dlt = ref_delta(b, block, half - 1 if d == 0 else half)
                pos = rt & (block - 1)
                is_target = (pos >= half) if d == 0 else (pos < half)
                x = (jnp.where(is_target, qs, kk) * jnp.exp2(-jnp.abs(dlt))).astype(BF16)
                att = jnp.where(level_masks[d][block], _nt_dot(x, x), att)
            atts.append(att.astype(BF16))
        o = _dot(jnp.concatenate(atts, axis=1), jnp.concatenate([v, v], axis=0))
        for d in (0, 1):
            o = o + _nt_dot(qh_scr[d, pl.ds(row, SLAB), :], sin_scr[d, slab])
        o_scr[pl.ds(row, SLAB), :] = o

    csrc = (qc_ref, afc_ref, abc_ref, vc_ref)
    lsrc = (q_ref, af_ref, ab_ref, v_ref)
    for s in range(nslab_c):
        gates_slab(csrc, s * SLAB, s)

    def gates_body(s, carry):
        gates_slab(lsrc, pl.multiple_of(s * SLAB, SLAB), s + nslab_c)
        return carry

    lax.fori_loop(0, t // SLAB, gates_body, 0, unroll=2)

    def ds_body(s, carry):
        row = pl.multiple_of(s * SLAB, SLAB)
        for d in (0, 1):
            ds_scr[d, s] = _dot(vt_scr[s], kb_scr[d, pl.ds(row, SLAB), :])
        return carry

    lax.fori_loop(0, nslab, ds_body, 0, unroll=3)

    orders = (list(range(nslab)),
              list(range(nslab_c - 1, -1, -1)) + list(range(nslab - 1, nslab_c - 1, -1)))
    for d in (0, 1):
        st = jnp.zeros((LANE, LANE), F32)
        for slab in orders[d]:
            sin_scr[d, slab] = st.astype(BF16)
            st = g_scr[d, slab:slab + 1, :] * st + ds_scr[d, slab]

    def mix_body(s, carry):
        mix_slab(s)
        return carry

    lax.fori_loop(0, nslab, mix_body, 0, unroll=6)

    def finish(r0, n, z_blk, out_ref):
        o = o_scr[r0:r0 + n, :]
        ms = jnp.mean(o * o, axis=-1, keepdims=True)
        on = o * lax.rsqrt(ms + EPS) * gn_ref[...]
        out_ref[...] = (on * _silu(z_blk.astype(F32))).astype(BF16)

    finish(0, m, zc_ref[0], uc_ref)
    finish(m, t, z_ref[0], u_ref)


def _hgrn_call(p, pc, lb_raw, g_norm, layer, nb, t, m, nh):
    depth = lb_raw.shape[1]
    tt = m + t
    assert t % (2 * SLAB) == 0 and m % SLAB == 0 and (tt // SLAB) % 6 == 0
    lat = lambda k: pl.BlockSpec((1, t, LANE), lambda b, h, k=k: (k * nh + h, b, 0))
    ctx = lambda k: pl.BlockSpec((1, m, LANE), lambda b, h, k=k: (k * nh + h, b, 0))
    return pl.pallas_call(
        functools.partial(_hgrn_kernel, layer=layer, t=t, m=m),
        grid=(nb, nh),
        in_specs=[lat(k) for k in range(5)] + [ctx(k) for k in range(5)] + [
            pl.BlockSpec((2, depth, LANE), lambda b, h: (0, 0, h)),
            pl.BlockSpec((1, LANE), lambda b, h: (0, h)),
            pl.BlockSpec((2, SLAB, 2 * SLAB), lambda b, h: (0, 0, 0)),
        ],
        out_specs=[
            pl.BlockSpec((t, LANE), lambda b, h: (b, h)),
            pl.BlockSpec((m, LANE), lambda b, h: (b, h)),
        ],
        out_shape=[
            jax.ShapeDtypeStruct((nb * t, nh * LANE), BF16),
            jax.ShapeDtypeStruct((nb * m, nh * LANE), BF16),
        ],
        scratch_shapes=[
            pltpu.VMEM((tt, LANE), F32),
            pltpu.VMEM((2, tt, LANE), BF16),
            pltpu.VMEM((2, tt, LANE), BF16),
            pltpu.VMEM((tt // SLAB, LANE, SLAB), BF16),
            pltpu.VMEM((2, tt // SLAB, LANE), F32),
            pltpu.VMEM((tt, LANE), F32),
            pltpu.VMEM((2, tt, LANE), F32),
            pltpu.VMEM((2, tt, LANE), F32),
            pltpu.VMEM((tt, LANE), BF16),
            pltpu.VMEM((2, tt // SLAB, LANE, LANE), F32),
            pltpu.VMEM((2, tt // SLAB, LANE, LANE), BF16),
        ],
        compiler_params=_params(("parallel", "parallel")),
        name="hgrn_scan",
    )(p, p, p, p, p, pc, pc, pc, pc, pc, lb_raw, g_norm.reshape(1, nh * LANE), _hgrn_tri())


def kernel(x, c, ctx, c_ctx, w_mod, b_mod, g_pre, g_post, na_w_in, na_rpb, na_w_out,
           pool_w_in, pool_w_grp, pool_scale, pool_w_out, hgrn_w_in, hgrn_lb, hgrn_gnorm, hgrn_w_out):
    nb, t, d = x.shape
    m = ctx.shape[1]
    depth = w_mod.shape[0]
    nh = na_w_out.shape[1] // LANE
    rows = t // GRID_W

    mods = _mod_call(c, c_ctx, w_mod, b_mod)
    w_ins = (na_w_in, pool_w_in, hgrn_w_in)
    w_outs = tuple(w.astype(BF16) for w in (na_w_out, pool_w_out, hgrn_w_out))
    xl = x.reshape(nb * t, d)
    xc = ctx.reshape(nb * m, d)

    for i in range(depth):
        kind, j = i % 3, i // 3
        need_ctx = i < depth - 1
        mod = mods[i]
        shift, scale, gate = (mod[:nb, None, k * d:(k + 1) * d] for k in range(3))
        shift_c, scale_c, gate_c = (mod[nb:nb + 1, None, k * d:(k + 1) * d] for k in range(3))
        w_in, w_out = w_ins[kind], w_outs[kind]

        p = _proj_call(xl, g_pre[i], scale, shift, w_in, j, t, f"proj_lat_{i}")
        pc = _proj_call(xc, g_pre[i], scale_c, shift_c, w_in, j, nb * m, f"proj_ctx_{i}")

        uc = None
        if kind == 0:
            u = _na_call(p, pc, na_rpb[j], nb, t, m, nh)
            if need_ctx:
                uc = _ctx_attn_call(pc, nb, m, nh)
        elif kind == 1:
            wg = pool_w_grp[j].astype(BF16)
            u = _pool_call(p, wg, pool_scale[j], nb, t)
            if need_ctx:
                uc = _pool_call(pc, wg, pool_scale[j], nb, m)
        else:
            u, uc = _hgrn_call(p, pc, hgrn_lb, hgrn_gnorm[j], i, nb, t, m, nh)

        xl = _out_call(u, w_out, j, xl, gate, g_post[i], t, f"out_lat_{i}")
        if need_ctx:
            xc = _out_call(uc, w_out, j, xc, gate_c, g_post[i], nb * m, f"out_ctx_{i}")

    return xl.reshape(nb, t, d)
```

```python
import functools

import numpy as np
import jax
import jax.numpy as jnp
from jax import lax
from jax.experimental import pallas as pl
from jax.experimental.pallas import tpu as pltpu

F32 = jnp.float32
BF16 = jnp.bfloat16

LANE = 128
VMEM_LIMIT = 56 * 1024 * 1024

EPS = 1e-6
GRID_W = 64
WIN_H = 8
WIN_W = 16
POOL_SIZES = (2, 4, 8, 16)
NEG = -1e30

NA_QROWS = 4
NA_HEADS = 4
LOG2E = 1.4426950408889634
NA_BAND = NA_QROWS + WIN_H - 1
SLAB = 128
PROJ_TM, PROJ_TN = 1024, 1024
OUT_TM = 512


def _silu(x):
    return x / (1.0 + jnp.exp(-x))


def _nt_dot(a, b):
    return lax.dot_general(a, b, (((1,), (1,)), ((), ())), preferred_element_type=F32)


def _dot(a, b):
    return jnp.dot(a, b, preferred_element_type=F32)


def _params(sem, vmem=VMEM_LIMIT):
    return pltpu.CompilerParams(dimension_semantics=sem, vmem_limit_bytes=vmem)


def _mod_kernel(c_ref, w_ref, b_ref, o_ref):
    s = _silu(c_ref[...])
    s_hi = s.astype(BF16)
    s_lo = (s - s_hi.astype(F32)).astype(BF16)
    w = w_ref[0]
    w_hi = w.astype(BF16)
    w_lo = (w - w_hi.astype(F32)).astype(BF16)
    acc = _dot(s_hi, w_hi) + _dot(s_lo, w_hi) + _dot(s_hi, w_lo)
    o_ref[0] = acc + b_ref[0]


def _mod_call(c, c_ctx, w_mod, b_mod):
    depth, d, n3 = w_mod.shape
    nb = c.shape[0]
    assert nb + 1 <= 8
    cs = jnp.zeros((8, d), F32).at[:nb].set(c).at[nb].set(c_ctx)
    tn = 1024 if n3 % 1024 == 0 else n3
    return pl.pallas_call(
        _mod_kernel,
        grid=(depth, n3 // tn),
        in_specs=[
            pl.BlockSpec((8, d), lambda l, j: (0, 0)),
            pl.BlockSpec((1, d, tn), lambda l, j: (l, 0, j)),
            pl.BlockSpec((1, 1, tn), lambda l, j: (l, 0, j)),
        ],
        out_specs=pl.BlockSpec((1, 8, tn), lambda l, j: (l, 0, j)),
        out_shape=jax.ShapeDtypeStruct((depth, 8, n3), F32),
        compiler_params=_params(("parallel", "parallel")),
        name="mod_vectors",
    )(cs, w_mod, b_mod.reshape(depth, 1, n3))


def _proj_kernel(x_ref, g_ref, sc_ref, sh_ref, w_ref, o_ref, h_ref, *, ncol):
    @pl.when(pl.program_id(1) == 0)
    def _():
        x = x_ref[...]
        ms = jnp.mean(x * x, axis=-1, keepdims=True)
        y = x * lax.rsqrt(ms + EPS) * g_ref[...]
        h_ref[...] = (y * (1.0 + sc_ref[0]) + sh_ref[0]).astype(BF16)

    acc = _dot(h_ref[...], w_ref[0].astype(BF16))
    for c in range(ncol):
        o_ref[c] = acc[:, c * LANE:(c + 1) * LANE].astype(BF16)


def _proj_call(x2d, g, scale, shift, w, layer, rows_per_batch, name, col0=0, ncols=None):
    r, d = x2d.shape
    n = w.shape[2] if ncols is None else ncols
    tm = min(PROJ_TM, rows_per_batch)
    tn = PROJ_TN if n % PROJ_TN == 0 and col0 % PROJ_TN == 0 else LANE
    assert r % tm == 0 and rows_per_batch % tm == 0 and n % tn == 0 and col0 % tn == 0
    ncol, j0 = tn // LANE, col0 // tn
    bidx = lambda i, j: ((i * tm) // rows_per_batch, 0, 0)
    return pl.pallas_call(
        functools.partial(_proj_kernel, ncol=ncol),
        grid=(r // tm, n // tn),
        in_specs=[
            pl.BlockSpec((tm, d), lambda i, j: (i, 0)),
            pl.BlockSpec((1, d), lambda i, j: (0, 0)),
            pl.BlockSpec((1, 1, d), bidx),
            pl.BlockSpec((1, 1, d), bidx),
            pl.BlockSpec((1, d, tn), lambda i, j: (layer, 0, j0 + j)),
        ],
        out_specs=pl.BlockSpec((ncol, tm, LANE), lambda i, j: (j, i, 0)),
        out_shape=jax.ShapeDtypeStruct((n // LANE, r, LANE), BF16),
        scratch_shapes=[pltpu.VMEM((tm, d), BF16)],
        compiler_params=_params(("parallel", "arbitrary")),
        name=name,
    )(x2d, g.reshape(1, d), scale, shift, w)


def _out_kernel(u_ref, w_ref, x_ref, gate_ref, gp_ref, o_ref):
    y = _dot(u_ref[...], w_ref[0])
    ms = jnp.mean(y * y, axis=-1, keepdims=True)
    r = y * lax.rsqrt(ms + EPS) * gp_ref[...]
    o_ref[...] = x_ref[...] + gate_ref[0] * r


def _out_call(u, w, layer, x2d, gate, g_post, rows_per_batch, name):
    r, kdim = u.shape
    d = w.shape[2]
    tm = min(OUT_TM, rows_per_batch)
    assert r % tm == 0 and rows_per_batch % tm == 0
    return pl.pallas_call(
        _out_kernel,
        grid=(r // tm,),
        in_specs=[
            pl.BlockSpec((tm, kdim), lambda i: (i, 0)),
            pl.BlockSpec((1, kdim, d), lambda i: (layer, 0, 0), pipeline_mode=pl.Buffered(1)),
            pl.BlockSpec((tm, d), lambda i: (i, 0)),
            pl.BlockSpec((1, 1, d), lambda i: ((i * tm) // rows_per_batch, 0, 0)),
            pl.BlockSpec((1, d), lambda i: (0, 0)),
        ],
        out_specs=pl.BlockSpec((tm, d), lambda i: (i, 0)),
        out_shape=jax.ShapeDtypeStruct((r, d), F32),
        compiler_params=_params(("parallel",)),
        name=name,
    )(u, w, x2d, gate, g_post.reshape(1, d))


def _na_block_plan(rows):
    plans = []
    for qs, bs in ((0, 0), (NA_QROWS, 0), (rows - NA_QROWS, rows - NA_BAND)):
        rq = qs + np.arange(NA_QROWS)
        rk = bs + np.arange(NA_BAND)
        r0 = np.clip(rq - WIN_H // 2, 0, rows - WIN_H)
        row_in = (rk[None, :] >= r0[:, None]) & (rk[None, :] < r0[:, None] + WIN_H)
        plans.append(np.where(row_in, rk[None, :] - rq[:, None] + (WIN_H - 1), -1))
    return plans


def _na_build_bias(rpb_ref, hh, tb_scr, bias_scr, rows):
    nrel_w = 2 * WIN_W - 1
    lane = lax.broadcasted_iota(jnp.int32, (GRID_W, LANE), 1)
    cq = lax.broadcasted_iota(jnp.int32, (GRID_W, LANE), 0)
    ck = lane & (GRID_W - 1)
    didx = ck - cq + (WIN_W - 1)
    c0 = jnp.clip(cq - WIN_W // 2, 0, GRID_W - WIN_W)
    col_in = (ck >= c0) & (ck < c0 + WIN_W)
    for dr in range(2 * WIN_H - 1):
        tile = jnp.zeros((GRID_W, LANE), F32)
        for j in range(nrel_w):
            tile = jnp.where(didx == j, rpb_ref[0, hh, dr * nrel_w + j] * LOG2E, tile)
        tb_scr[hh, dr] = jnp.where(col_in, tile, NEG)
    neg = jnp.full((GRID_W, LANE), NEG, F32)
    for var, plan in enumerate(_na_block_plan(rows)):
        for rq in range(NA_QROWS):
            for pair in range((NA_BAND + 1) // 2):
                rks = [rk for rk in (2 * pair, 2 * pair + 1) if rk < NA_BAND]
                src = [tb_scr[hh, int(plan[rq, rk])] if plan[rq, rk] >= 0 else neg for rk in rks]
                tile = src[0] if len(src) == 1 else jnp.where(lane < GRID_W, src[0], src[1])
                width = GRID_W * len(rks)
                bias_scr[hh, var, rq * GRID_W:(rq + 1) * GRID_W,
                         pair * LANE:pair * LANE + width] = tile[:, :width]


def _na_kernel(rpb_ref, q_ref, k_ref, v_ref, z_ref, kc_ref, vc_ref, o_ref, tb_scr, bias_scr,
               *, rows, heads):
    nqb = rows // NA_QROWS
    tq = NA_QROWS * GRID_W
    nkeys = NA_BAND * GRID_W

    @pl.when(pl.program_id(1) == 0)
    def _():
        for hh in range(heads):
            _na_build_bias(rpb_ref, hh, tb_scr, bias_scr, rows)

    def query_block(qb, carry):
        var = jnp.where(qb == 0, 0, jnp.where(qb == nqb - 1, 2, 1))
        rs = jnp.clip(qb * NA_QROWS - WIN_H // 2, 0, rows - NA_BAND)
        start = pl.multiple_of(rs * GRID_W, GRID_W)
        row = pl.multiple_of(qb * tq, tq)
        for hh in range(heads):
            q = (q_ref[hh, pl.ds(row, tq), :].astype(F32) * (LANE ** -0.5 * LOG2E)).astype(BF16)
            kb = k_ref[hh, pl.ds(start, nkeys), :]
            vb = v_ref[hh, pl.ds(start, nkeys), :]
            s1 = _nt_dot(q, kb) + bias_scr[hh, var]
            s2 = _nt_dot(q, kc_ref[hh])
            mx = jnp.maximum(jnp.max(s1, axis=-1, keepdims=True), jnp.max(s2, axis=-1, keepdims=True))
            p1 = jnp.exp2(s1 - mx)
            p2 = jnp.exp2(s2 - mx)
            l = jnp.sum(p1, axis=-1, keepdims=True) + jnp.sum(p2, axis=-1, keepdims=True)
            o = (_dot(p1.astype(BF16), vb) + _dot(p2.astype(BF16), vc_ref[hh])) / l
            z = z_ref[hh, pl.ds(row, tq), :].astype(F32)
            o_ref[pl.ds(row, tq), hh * LANE:(hh + 1) * LANE] = (o * _silu(z)).astype(BF16)
        return carry

    lax.fori_loop(0, nqb, query_block, 0, unroll=2)


def _na_call(p, pc, rpb, nb, t, m, nh, ctx_first=0):
    rows = t // GRID_W
    nq, nk = NA_QROWS * GRID_W, NA_BAND * GRID_W
    nrel = rpb.shape[1] * rpb.shape[2]
    hb = NA_HEADS if nh % NA_HEADS == 0 else 1
    ng = nh // hb
    lat = lambda k: pl.BlockSpec((hb, t, LANE), lambda h, b, k=k: (k * ng + h, b, 0))
    ctx = lambda k: pl.BlockSpec((hb, m, LANE), lambda h, b, k=k: ((k - ctx_first) * ng + h, b, 0))
    return pl.pallas_call(
        functools.partial(_na_kernel, rows=rows, heads=hb),
        grid=(ng, nb),
        in_specs=[
            pl.BlockSpec((1, hb, nrel), lambda h, b: (h, 0, 0), memory_space=pltpu.SMEM),
            lat(0), lat(1), lat(2), lat(3), ctx(1), ctx(2),
        ],
        out_specs=pl.BlockSpec((t, hb * LANE), lambda h, b: (b, h)),
        out_shape=jax.ShapeDtypeStruct((nb * t, nh * LANE), BF16),
        scratch_shapes=[
            pltpu.VMEM((hb, 2 * WIN_H - 1, GRID_W, LANE), F32),
            pltpu.VMEM((hb, 3, nq, nk), F32),
        ],
        compiler_params=_params(("parallel", "arbitrary")),
        name="na_attention",
    )(rpb.reshape(ng, hb, nrel), p, p, p, p, pc, pc)


def _ctx_attn_kernel(q_ref, k_ref, v_ref, z_ref, o_ref, *, heads):
    for hh in range(heads):
        q = (q_ref[hh].astype(F32) * (LANE ** -0.5)).astype(BF16)
        s = _nt_dot(q, k_ref[hh])
        mx = jnp.max(s, axis=-1, keepdims=True)
        p = jnp.exp(s - mx)
        l = jnp.sum(p, axis=-1, keepdims=True)
        o = _dot(p.astype(BF16), v_ref[hh]) / l
        o_ref[:, hh * LANE:(hh + 1) * LANE] = (o * _silu(z_ref[hh].astype(F32))).astype(BF16)


def _ctx_attn_call(pc, nb, m, nh):
    hb = NA_HEADS if nh % NA_HEADS == 0 else 1
    ng = nh // hb
    blk = lambda k: pl.BlockSpec((hb, m, LANE), lambda h, b, k=k: (k * ng + h, b, 0))
    return pl.pallas_call(
        functools.partial(_ctx_attn_kernel, heads=hb),
        grid=(ng, nb),
        in_specs=[blk(0), blk(1), blk(2), blk(3)],
        out_specs=pl.BlockSpec((m, hb * LANE), lambda h, b: (b, h)),
        out_shape=jax.ShapeDtypeStruct((nb * m, nh * LANE), BF16),
        compiler_params=_params(("parallel", "parallel")),
        name="ctx_attention",
    )(pc, pc, pc, pc)


def _pool_kernel(u_ref, w_ref, z_ref, s_ref, o_ref, lhs_scr, *, n, kcol, ncol):
    group = pl.program_id(0)
    t = lax.broadcasted_iota(jnp.int32, (n, LANE), 0)

    def up(a, k):
        return jnp.where(t < n - k, pltpu.roll(a, n - k, axis=0), 0.0)

    def down(a, k):
        return jnp.where(t >= k, pltpu.roll(a, k, axis=0), 0.0)

    for gi, w in enumerate(POOL_SIZES):
        half = w // 2

        @pl.when((group == gi) & (pl.program_id(2) == 0))
        def _(half=half):
            cnt = (jnp.minimum(t + half, n) - jnp.maximum(t - half, 0)).astype(F32)
            for c in range(kcol):
                x = u_ref[c].astype(F32)
                fwd, bwd, k = x, x, 1
                while k < half:
                    fwd = fwd + up(fwd, k)
                    bwd = bwd + down(bwd, k)
                    k *= 2
                total = fwd + down(bwd, 1)
                lhs_scr[:, c * LANE:(c + 1) * LANE] = (total / cnt - x).astype(BF16)

    acc = _dot(lhs_scr[...], w_ref[0])
    z = jnp.concatenate([z_ref[c] for c in range(ncol)], axis=-1).astype(F32)
    o_ref[...] = (acc * s_ref[...] * _silu(z)).astype(BF16)


def _pool_call(p, w_grp, scale, nb, n):
    ng, cg, _ = w_grp.shape
    assert ng == len(POOL_SIZES)
    tn = min(512, cg)
    kcol, ncol, nj = cg // LANE, tn // LANE, cg // tn
    zoff = (ng * cg) // tn
    return pl.pallas_call(
        functools.partial(_pool_kernel, n=n, kcol=kcol, ncol=ncol),
        grid=(ng, nb, nj),
        in_specs=[
            pl.BlockSpec((kcol, n, LANE), lambda g, b, j: (g, b, 0)),
            pl.BlockSpec((1, cg, tn), lambda g, b, j: (g, 0, j)),
            pl.BlockSpec((ncol, n, LANE), lambda g, b, j: (zoff + g * nj + j, b, 0)),
            pl.BlockSpec((1, tn), lambda g, b, j: (0, g * nj + j)),
        ],
        out_specs=pl.BlockSpec((n, tn), lambda g, b, j: (b, g * nj + j)),
        out_shape=jax.ShapeDtypeStruct((nb * n, ng * cg), BF16),
        scratch_shapes=[pltpu.VMEM((n, cg), BF16)],
        compiler_params=_params(("parallel", "parallel", "arbitrary")),
        name="pool_mixer",
    )(p, w_grp, p, scale.reshape(1, ng * cg))


HGRN_LEVELS = (128, 64, 32, 16)
HGRN_DIAG = 8


def _hgrn_tri():
    t = np.arange(SLAB)
    fwd = (t[None, :] <= t[:, None]).astype(np.float32)
    tri = np.stack([np.concatenate([fwd, fwd], axis=1), np.concatenate([fwd.T, fwd.T], axis=1)])
    return jnp.asarray(tri, BF16)


def _hgrn_kernel(q_ref, af_ref, ab_ref, v_ref, z_ref, qc_ref, afc_ref, abc_ref, vc_ref, zc_ref,
                 lb_ref, gn_ref, tri_ref, u_ref, uc_ref,
                 o_scr, qh_scr, kb_scr, vt_scr, g_scr, qs_scr, kk_scr, b_scr, v_scr, ds_scr, sin_scr,
                 *, layer, t, m):
    nslab_c = m // SLAB
    nslab = (m + t) // SLAB

    lbs = []
    for d in (0, 1):
        raw = lb_ref[d]
        e = jnp.exp(raw - jnp.max(raw, axis=0, keepdims=True))
        probs = e / jnp.sum(e, axis=0, keepdims=True)
        lbs.append(jnp.sum(probs[1:layer + 1], axis=0, keepdims=True) if layer >= 1
                   else jnp.zeros((1, LANE), F32))

    ti = lax.broadcasted_iota(jnp.int32, (SLAB, SLAB), 0)
    si = lax.broadcasted_iota(jnp.int32, (SLAB, SLAB), 1)
    rt = lax.broadcasted_iota(jnp.int32, (SLAB, LANE), 0)
    diag_masks, level_masks = [], []
    for d in (0, 1):
        before = (si < ti) if d == 0 else (si > ti)
        diag_masks.append(((ti // HGRN_DIAG) == (si // HGRN_DIAG)) & (before | (si == ti)))
        level_masks.append({blk: ((ti // blk) == (si // blk)) & ((ti // (blk // 2)) != (si // (blk // 2)))
                            & before for blk in HGRN_LEVELS})

    def ref_delta(b, block, ref_row):
        b3 = b.reshape(SLAB // block, block, LANE)
        return (b3 - b3[:, ref_row:ref_row + 1, :]).reshape(SLAB, LANE)

    def gates_slab(srcs, r0, slab):
        row = slab * SLAB
        qs = _silu(srcs[0][0, pl.ds(r0, SLAB), :].astype(F32))
        qs_scr[pl.ds(row, SLAB), :] = qs
        v = srcs[3][0, pl.ds(r0, SLAB), :]
        v_scr[pl.ds(row, SLAB), :] = v
        vt_scr[slab] = v.astype(F32).T.astype(BF16)
        for d in (0, 1):
            a = srcs[1 + d][0, pl.ds(r0, SLAB), :].astype(F32)
            f = lbs[d] + (1.0 - lbs[d]) * (1.0 / (1.0 + jnp.exp(-a)))
            lf = jnp.log2(f)
            kk = 1.0 - f
            hi = lf.astype(BF16)
            lo = (lf - hi.astype(F32)).astype(BF16)
            b = _dot(tri_ref[d], jnp.concatenate([hi, lo], axis=0))
            b_end = b[SLAB - 1:SLAB] if d == 0 else b[0:1]
            kk_scr[d, pl.ds(row, SLAB), :] = kk
            b_scr[d, pl.ds(row, SLAB), :] = b
            qh_scr[d, pl.ds(row, SLAB), :] = (qs * jnp.exp2(b)).astype(BF16)
            kb_scr[d, pl.ds(row, SLAB), :] = (kk * jnp.exp2(b_end - b)).astype(BF16)
            g_scr[d, pl.ds(slab, 1), :] = jnp.exp2(b_end)

    def mix_slab(slab):
        row = pl.multiple_of(slab * SLAB, SLAB)
        qs = qs_scr[pl.ds(row, SLAB), :]
        v = v_scr[pl.ds(row, SLAB), :]
        atts = []
        for d in (0, 1):
            kk = kk_scr[d, pl.ds(row, SLAB), :]
            b = b_scr[d, pl.ds(row, SLAB), :]
            dlt = ref_delta(b, HGRN_DIAG, HGRN_DIAG // 2 - 1 if d == 0 else HGRN_DIAG // 2)
            qt = (qs * jnp.exp2(dlt)).astype(BF16)
            kt = (kk * jnp.exp2(-dlt)).astype(BF16)
            att = jnp.where(diag_masks[d], _nt_dot(qt, kt), 0.0)
            for block in HGRN_LEVELS:
                half = block // 2
                dlt = ref_delta(b, block, half - 1 if d == 0 else half)
                pos = rt & (block - 1)
                is_target = (pos >= half) if d == 0 else (pos < half)
                x = (jnp.where(is_target, qs, kk) * jnp.exp2(-jnp.abs(dlt))).astype(BF16)
                att = jnp.where(level_masks[d][block], _nt_dot(x, x), att)
            atts.append(att.astype(BF16))
        o = _dot(jnp.concatenate(atts, axis=1), jnp.concatenate([v, v], axis=0))
        for d in (0, 1):
            o = o + _nt_dot(qh_scr[d, pl.ds(row, SLAB), :], sin_scr[d, slab])
        o_scr[pl.ds(row, SLAB), :] = o

    csrc = (qc_ref, afc_ref, abc_ref, vc_ref)
    lsrc = (q_ref, af_ref, ab_ref, v_ref)
    for s in range(nslab_c):
        gates_slab(csrc, s * SLAB, s)

    def gates_body(s, carry):
        gates_slab(lsrc, pl.multiple_of(s * SLAB, SLAB), s + nslab_c)
        return carry

    lax.fori_loop(0, t // SLAB, gates_body, 0, unroll=4)

    def ds_body(s, carry):
        row = pl.multiple_of(s * SLAB, SLAB)
        for d in (0, 1):
            ds_scr[d, s] = _dot(vt_scr[s], kb_scr[d, pl.ds(row, SLAB), :])
        return carry

    lax.fori_loop(0, nslab, ds_body, 0, unroll=3)

    orders = (list(range(nslab)),
              list(range(nslab_c - 1, -1, -1)) + list(range(nslab - 1, nslab_c - 1, -1)))
    for d in (0, 1):
        st = jnp.zeros((LANE, LANE), F32)
        for slab in orders[d]:
            sin_scr[d, slab] = st.astype(BF16)
            st = g_scr[d, slab:slab + 1, :] * st + ds_scr[d, slab]

    def mix_body(s, carry):
        mix_slab(s)
        return carry

    lax.fori_loop(0, nslab, mix_body, 0, unroll=6)

    def finish(r0, n, z_blk, out_ref):
        o = o_scr[r0:r0 + n, :]
        ms = jnp.mean(o * o, axis=-1, keepdims=True)
        on = o * lax.rsqrt(ms + EPS) * gn_ref[...]
        out_ref[...] = (on * _silu(z_blk.astype(F32))).astype(BF16)

    finish(0, m, zc_ref[0], uc_ref)
    finish(m, t, z_ref[0], u_ref)


def _hgrn_call(p, pc, lb_raw, g_norm, layer, nb, t, m, nh):
    depth = lb_raw.shape[1]
    tt = m + t
    assert t % (4 * SLAB) == 0 and m % SLAB == 0 and (tt // SLAB) % 6 == 0
    lat = lambda k: pl.BlockSpec((1, t, LANE), lambda b, h, k=k: (k * nh + h, b, 0))
    ctx = lambda k: pl.BlockSpec((1, m, LANE), lambda b, h, k=k: (k * nh + h, b, 0))
    return pl.pallas_call(
        functools.partial(_hgrn_kernel, layer=layer, t=t, m=m),
        grid=(nb, nh),
        in_specs=[lat(k) for k in range(5)] + [ctx(k) for k in range(5)] + [
            pl.BlockSpec((2, depth, LANE), lambda b, h: (0, 0, h)),
            pl.BlockSpec((1, LANE), lambda b, h: (0, h)),
            pl.BlockSpec((2, SLAB, 2 * SLAB), lambda b, h: (0, 0, 0)),
        ],
        out_specs=[
            pl.BlockSpec((t, LANE), lambda b, h: (b, h)),
            pl.BlockSpec((m, LANE), lambda b, h: (b, h)),
        ],
        out_shape=[
            jax.ShapeDtypeStruct((nb * t, nh * LANE), BF16),
            jax.ShapeDtypeStruct((nb * m, nh * LANE), BF16),
        ],
        scratch_shapes=[
            pltpu.VMEM((tt, LANE), F32),
            pltpu.VMEM((2, tt, LANE), BF16),
            pltpu.VMEM((2, tt, LANE), BF16),
            pltpu.VMEM((tt // SLAB, LANE, SLAB), BF16),
            pltpu.VMEM((2, tt // SLAB, LANE), F32),
            pltpu.VMEM((tt, LANE), F32),
            pltpu.VMEM((2, tt, LANE), F32),
            pltpu.VMEM((2, tt, LANE), F32),
            pltpu.VMEM((tt, LANE), BF16),
            pltpu.VMEM((2, tt // SLAB, LANE, LANE), F32),
            pltpu.VMEM((2, tt // SLAB, LANE, LANE), BF16),
        ],
        compiler_params=_params(("parallel", "parallel")),
        name="hgrn_scan",
    )(p, p, p, p, p, pc, pc, pc, pc, pc, lb_raw, g_norm.reshape(1, nh * LANE), _hgrn_tri())


def kernel(x, c, ctx, c_ctx, w_mod, b_mod, g_pre, g_post, na_w_in, na_rpb, na_w_out,
           pool_w_in, pool_w_grp, pool_scale, pool_w_out, hgrn_w_in, hgrn_lb, hgrn_gnorm, hgrn_w_out):
    nb, t, d = x.shape
    m = ctx.shape[1]
    depth = w_mod.shape[0]
    nh = na_w_out.shape[1] // LANE
    rows = t // GRID_W

    mods = _mod_call(c, c_ctx, w_mod, b_mod)
    w_ins = (na_w_in, pool_w_in, hgrn_w_in)
    w_outs = tuple(w.astype(BF16) for w in (na_w_out, pool_w_out, hgrn_w_out))
    xl = x.reshape(nb * t, d)
    xc = ctx.reshape(nb * m, d)

    for i in range(depth):
        kind, j = i % 3, i // 3
        need_ctx = i < depth - 1
        mod = mods[i]
        shift, scale, gate = (mod[:nb, None, k * d:(k + 1) * d] for k in range(3))
        shift_c, scale_c, gate_c = (mod[nb:nb + 1, None, k * d:(k + 1) * d] for k in range(3))
        w_in, w_out = w_ins[kind], w_outs[kind]

        p = _proj_call(xl, g_pre[i], scale, shift, w_in, j, t, f"proj_lat_{i}")
        if kind == 0 and not need_ctx:
            pc = _proj_call(xc, g_pre[i], scale_c, shift_c, w_in, j, nb * m, f"proj_ctx_{i}",
                            col0=nh * LANE, ncols=2 * nh * LANE)
        else:
            pc = _proj_call(xc, g_pre[i], scale_c, shift_c, w_in, j, nb * m, f"proj_ctx_{i}")

        uc = None
        if kind == 0:
            u = _na_call(p, pc, na_rpb[j], nb, t, m, nh, ctx_first=0 if need_ctx else 1)
            if need_ctx:
                uc = _ctx_attn_call(pc, nb, m, nh)
        elif kind == 1:
            wg = pool_w_grp[j].astype(BF16)
            u = _pool_call(p, wg, pool_scale[j], nb, t)
            if need_ctx:
                uc = _pool_call(pc, wg, pool_scale[j], nb, m)
        else:
            u, uc = _hgrn_call(p, pc, hgrn_lb, hgrn_gnorm[j], i, nb, t, m, nh)

        xl = _out_call(u, w_out, j, xl, gate, g_post[i], t, f"out_lat_{i}")
        if need_ctx:
            xc = _out_call(uc, w_out, j, xc, gate_c, g_post[i], nb * m, f"out_ctx_{i}")

    return xl.reshape(nb, t, d)
```

```python
import functools

import numpy as np
import jax
import jax.numpy as jnp
from jax import lax
from jax.experimental import pallas as pl
from jax.experimental.pallas import tpu as pltpu

F32 = jnp.float32
BF16 = jnp.bfloat16

LANE = 128
VMEM_LIMIT = 56 * 1024 * 1024

EPS = 1e-6
GRID_W = 64
WIN_H = 8
WIN_W = 16
POOL_SIZES = (2, 4, 8, 16)
NEG = -1e30

NA_QROWS = 4
NA_HEADS = 4
LOG2E = 1.4426950408889634
NA_BAND = NA_QROWS + WIN_H - 1
SLAB = 128
PROJ_TM, PROJ_TN = 1024, 1024
OUT_TM = 512


def _silu(x):
    return x / (1.0 + jnp.exp(-x))


def _nt_dot(a, b):
    return lax.dot_general(a, b, (((1,), (1,)), ((), ())), preferred_element_type=F32)


def _dot(a, b):
    return jnp.dot(a, b, preferred_element_type=F32)


def _params(sem, vmem=VMEM_LIMIT):
    return pltpu.CompilerParams(dimension_semantics=sem, vmem_limit_bytes=vmem)


def _mod_kernel(c_ref, w_ref, b_ref, o_ref):
    s = _silu(c_ref[...])
    s_hi = s.astype(BF16)
    s_lo = (s - s_hi.astype(F32)).astype(BF16)
    w = w_ref[0]
    w_hi = w.astype(BF16)
    w_lo = (w - w_hi.astype(F32)).astype(BF16)
    acc = _dot(s_hi, w_hi) + _dot(s_lo, w_hi) + _dot(s_hi, w_lo)
    o_ref[0] = acc + b_ref[0]


def _mod_call(c, c_ctx, w_mod, b_mod):
    depth, d, n3 = w_mod.shape
    nb = c.shape[0]
    assert nb + 1 <= 8
    cs = jnp.zeros((8, d), F32).at[:nb].set(c).at[nb].set(c_ctx)
    tn = 1024 if n3 % 1024 == 0 else n3
    return pl.pallas_call(
        _mod_kernel,
        grid=(depth, n3 // tn),
        in_specs=[
            pl.BlockSpec((8, d), lambda l, j: (0, 0)),
            pl.BlockSpec((1, d, tn), lambda l, j: (l, 0, j)),
            pl.BlockSpec((1, 1, tn), lambda l, j: (l, 0, j)),
        ],
        out_specs=pl.BlockSpec((1, 8, tn), lambda l, j: (l, 0, j)),
        out_shape=jax.ShapeDtypeStruct((depth, 8, n3), F32),
        compiler_params=_params(("parallel", "parallel")),
        name="mod_vectors",
    )(cs, w_mod, b_mod.reshape(depth, 1, n3))


def _proj_kernel(x_ref, g_ref, sc_ref, sh_ref, w_ref, o_ref, h_ref, *, ncol):
    @pl.when(pl.program_id(1) == 0)
    def _():
        x = x_ref[...]
        ms = jnp.mean(x * x, axis=-1, keepdims=True)
        y = x * lax.rsqrt(ms + EPS) * g_ref[...]
        h_ref[...] = (y * (1.0 + sc_ref[0]) + sh_ref[0]).astype(BF16)

    acc = _dot(h_ref[...], w_ref[0].astype(BF16))
    for c in range(ncol):
        o_ref[c] = acc[:, c * LANE:(c + 1) * LANE].astype(BF16)


def _proj_call(x2d, g, scale, shift, w, layer, rows_per_batch, name, col0=0, ncols=None):
    r, d = x2d.shape
    n = w.shape[2] if ncols is None else ncols
    tm = min(PROJ_TM, rows_per_batch)
    tn = PROJ_TN if n % PROJ_TN == 0 and col0 % PROJ_TN == 0 else LANE
    assert r % tm == 0 and rows_per_batch % tm == 0 and n % tn == 0 and col0 % tn == 0
    ncol, j0 = tn // LANE, col0 // tn
    bidx = lambda i, j: ((i * tm) // rows_per_batch, 0, 0)
    return pl.pallas_call(
        functools.partial(_proj_kernel, ncol=ncol),
        grid=(r // tm, n // tn),
        in_specs=[
            pl.BlockSpec((tm, d), lambda i, j: (i, 0)),
            pl.BlockSpec((1, d), lambda i, j: (0, 0)),
            pl.BlockSpec((1, 1, d), bidx),
            pl.BlockSpec((1, 1, d), bidx),
            pl.BlockSpec((1, d, tn), lambda i, j: (layer, 0, j0 + j)),
        ],
        out_specs=pl.BlockSpec((ncol, tm, LANE), lambda i, j: (j, i, 0)),
        out_shape=jax.ShapeDtypeStruct((n // LANE, r, LANE), BF16),
        scratch_shapes=[pltpu.VMEM((tm, d), BF16)],
        compiler_params=_params(("parallel", "arbitrary")),
        name=name,
    )(x2d, g.reshape(1, d), scale, shift, w)


def _out_kernel(u_ref, w_ref, x_ref, gate_ref, gp_ref, o_ref):
    y = _dot(u_ref[...], w_ref[0])
    ms = jnp.mean(y * y, axis=-1, keepdims=True)
    r = y * lax.rsqrt(ms + EPS) * gp_ref[...]
    o_ref[...] = x_ref[...] + gate_ref[0] * r


def _out_call(u, w, layer, x2d, gate, g_post, rows_per_batch, name):
    r, kdim = u.shape
    d = w.shape[2]
    tm = min(OUT_TM, rows_per_batch)
    assert r % tm == 0 and rows_per_batch % tm == 0
    return pl.pallas_call(
        _out_kernel,
        grid=(r // tm,),
        in_specs=[
            pl.BlockSpec((tm, kdim), lambda i: (i, 0)),
            pl.BlockSpec((1, kdim, d), lambda i: (layer, 0, 0), pipeline_mode=pl.Buffered(1)),
            pl.BlockSpec((tm, d), lambda i: (i, 0)),
            pl.BlockSpec((1, 1, d), lambda i: ((i * tm) // rows_per_batch, 0, 0)),
            pl.BlockSpec((1, d), lambda i: (0, 0)),
        ],
        out_specs=pl.BlockSpec((tm, d), lambda i: (i, 0)),
        out_shape=jax.ShapeDtypeStruct((r, d), F32),
        compiler_params=_params(("parallel",)),
        name=name,
    )(u, w, x2d, gate, g_post.reshape(1, d))


def _na_block_plan(rows):
    plans = []
    for qs, bs in ((0, 0), (NA_QROWS, 0), (rows - NA_QROWS, rows - NA_BAND)):
        rq = qs + np.arange(NA_QROWS)
        rk = bs + np.arange(NA_BAND)
        r0 = np.clip(rq - WIN_H // 2, 0, rows - WIN_H)
        row_in = (rk[None, :] >= r0[:, None]) & (rk[None, :] < r0[:, None] + WIN_H)
        plans.append(np.where(row_in, rk[None, :] - rq[:, None] + (WIN_H - 1), -1))
    return plans


def _na_build_bias(rpb_ref, hh, tb_scr, bias_scr, rows):
    nrel_w = 2 * WIN_W - 1
    lane = lax.broadcasted_iota(jnp.int32, (GRID_W, LANE), 1)
    cq = lax.broadcasted_iota(jnp.int32, (GRID_W, LANE), 0)
    ck = lane & (GRID_W - 1)
    didx = ck - cq + (WIN_W - 1)
    c0 = jnp.clip(cq - WIN_W // 2, 0, GRID_W - WIN_W)
    col_in = (ck >= c0) & (ck < c0 + WIN_W)
    for dr in range(2 * WIN_H - 1):
        tile = jnp.zeros((GRID_W, LANE), F32)
        for j in range(nrel_w):
            tile = jnp.where(didx == j, rpb_ref[0, hh, dr * nrel_w + j] * LOG2E, tile)
        tb_scr[hh, dr] = jnp.where(col_in, tile, NEG)
    neg = jnp.full((GRID_W, LANE), NEG, F32)
    for var, plan in enumerate(_na_block_plan(rows)):
        for rq in range(NA_QROWS):
            for pair in range((NA_BAND + 1) // 2):
                rks = [rk for rk in (2 * pair, 2 * pair + 1) if rk < NA_BAND]
                src = [tb_scr[hh, int(plan[rq, rk])] if plan[rq, rk] >= 0 else neg for rk in rks]
                tile = src[0] if len(src) == 1 else jnp.where(lane < GRID_W, src[0], src[1])
                width = GRID_W * len(rks)
                bias_scr[hh, var, rq * GRID_W:(rq + 1) * GRID_W,
                         pair * LANE:pair * LANE + width] = tile[:, :width]


def _na_kernel(rpb_ref, q_ref, k_ref, v_ref, z_ref, kc_ref, vc_ref, o_ref, tb_scr, bias_scr,
               *, rows, heads):
    nqb = rows // NA_QROWS
    tq = NA_QROWS * GRID_W
    nkeys = NA_BAND * GRID_W

    @pl.when(pl.program_id(1) == 0)
    def _():
        for hh in range(heads):
            _na_build_bias(rpb_ref, hh, tb_scr, bias_scr, rows)

    def query_block(qb, carry):
        var = jnp.where(qb == 0, 0, jnp.where(qb == nqb - 1, 2, 1))
        rs = jnp.clip(qb * NA_QROWS - WIN_H // 2, 0, rows - NA_BAND)
        start = pl.multiple_of(rs * GRID_W, GRID_W)
        row = pl.multiple_of(qb * tq, tq)
        for hh in range(heads):
            q = (q_ref[hh, pl.ds(row, tq), :].astype(F32) * (LANE ** -0.5 * LOG2E)).astype(BF16)
            kb = k_ref[hh, pl.ds(start, nkeys), :]
            vb = v_ref[hh, pl.ds(start, nkeys), :]
            s1 = _nt_dot(q, kb) + bias_scr[hh, var]
            s2 = _nt_dot(q, kc_ref[hh])
            mx = jnp.maximum(jnp.max(s1, axis=-1, keepdims=True), jnp.max(s2, axis=-1, keepdims=True))
            p1 = jnp.exp2(s1 - mx)
            p2 = jnp.exp2(s2 - mx)
            l = jnp.sum(p1, axis=-1, keepdims=True) + jnp.sum(p2, axis=-1, keepdims=True)
            o = (_dot(p1.astype(BF16), vb) + _dot(p2.astype(BF16), vc_ref[hh])) / l
            z = z_ref[hh, pl.ds(row, tq), :].astype(F32)
            o_ref[pl.ds(row, tq), hh * LANE:(hh + 1) * LANE] = (o * _silu(z)).astype(BF16)
        return carry

    lax.fori_loop(0, nqb, query_block, 0, unroll=2)


def _na_call(p, pc, rpb, nb, t, m, nh, ctx_first=0):
    rows = t // GRID_W
    nq, nk = NA_QROWS * GRID_W, NA_BAND * GRID_W
    nrel = rpb.shape[1] * rpb.shape[2]
    hb = NA_HEADS if nh % NA_HEADS == 0 else 1
    ng = nh // hb
    lat = lambda k: pl.BlockSpec((hb, t, LANE), lambda h, b, k=k: (k * ng + h, b, 0))
    ctx = lambda k: pl.BlockSpec((hb, m, LANE), lambda h, b, k=k: ((k - ctx_first) * ng + h, b, 0))
    return pl.pallas_call(
        functools.partial(_na_kernel, rows=rows, heads=hb),
        grid=(ng, nb),
        in_specs=[
            pl.BlockSpec((1, hb, nrel), lambda h, b: (h, 0, 0), memory_space=pltpu.SMEM),
            lat(0), lat(1), lat(2), lat(3), ctx(1), ctx(2),
        ],
        out_specs=pl.BlockSpec((t, hb * LANE), lambda h, b: (b, h)),
        out_shape=jax.ShapeDtypeStruct((nb * t, nh * LANE), BF16),
        scratch_shapes=[
            pltpu.VMEM((hb, 2 * WIN_H - 1, GRID_W, LANE), F32),
            pltpu.VMEM((hb, 3, nq, nk), F32),
        ],
        compiler_params=_params(("parallel", "arbitrary")),
        name="na_attention",
    )(rpb.reshape(ng, hb, nrel), p, p, p, p, pc, pc)


def _ctx_attn_kernel(q_ref, k_ref, v_ref, z_ref, o_ref, *, heads):
    for hh in range(heads):
        q = (q_ref[hh].astype(F32) * (LANE ** -0.5)).astype(BF16)
        s = _nt_dot(q, k_ref[hh])
        mx = jnp.max(s, axis=-1, keepdims=True)
        p = jnp.exp(s - mx)
        l = jnp.sum(p, axis=-1, keepdims=True)
        o = _dot(p.astype(BF16), v_ref[hh]) / l
        o_ref[:, hh * LANE:(hh + 1) * LANE] = (o * _silu(z_ref[hh].astype(F32))).astype(BF16)


def _ctx_attn_call(pc, nb, m, nh):
    hb = NA_HEADS if nh % NA_HEADS == 0 else 1
    ng = nh // hb
    blk = lambda k: pl.BlockSpec((hb, m, LANE), lambda h, b, k=k: (k * ng + h, b, 0))
    return pl.pallas_call(
        functools.partial(_ctx_attn_kernel, heads=hb),
        grid=(ng, nb),
        in_specs=[blk(0), blk(1), blk(2), blk(3)],
        out_specs=pl.BlockSpec((m, hb * LANE), lambda h, b: (b, h)),
        out_shape=jax.ShapeDtypeStruct((nb * m, nh * LANE), BF16),
        compiler_params=_params(("parallel", "parallel")),
        name="ctx_attention",
    )(pc, pc, pc, pc)


def _pool_kernel(u_ref, w_ref, z_ref, s_ref, o_ref, lhs_scr, *, n, kcol, ncol):
    group = pl.program_id(0)
    t = lax.broadcasted_iota(jnp.int32, (n, LANE), 0)

    def up(a, k):
        return jnp.where(t < n - k, pltpu.roll(a, n - k, axis=0), 0.0)

    def down(a, k):
        return jnp.where(t >= k, pltpu.roll(a, k, axis=0), 0.0)

    for gi, w in enumerate(POOL_SIZES):
        half = w // 2

        @pl.when((group == gi) & (pl.program_id(2) == 0))
        def _(half=half):
            cnt = (jnp.minimum(t + half, n) - jnp.maximum(t - half, 0)).astype(F32)
            for c in range(kcol):
                x = u_ref[c].astype(F32)
                fwd, bwd, k = x, x, 1
                while k < half:
                    fwd = fwd + up(fwd, k)
                    bwd = bwd + down(bwd, k)
                    k *= 2
                total = fwd + down(bwd, 1)
                lhs_scr[:, c * LANE:(c + 1) * LANE] = (total / cnt - x).astype(BF16)

    acc = _dot(lhs_scr[...], w_ref[0])
    z = jnp.concatenate([z_ref[c] for c in range(ncol)], axis=-1).astype(F32)
    o_ref[...] = (acc * s_ref[...] * _silu(z)).astype(BF16)


def _pool_call(p, w_grp, scale, nb, n):
    ng, cg, _ = w_grp.shape
    assert ng == len(POOL_SIZES)
    tn = min(512, cg)
    kcol, ncol, nj = cg // LANE, tn // LANE, cg // tn
    zoff = (ng * cg) // tn
    return pl.pallas_call(
        functools.partial(_pool_kernel, n=n, kcol=kcol, ncol=ncol),
        grid=(ng, nb, nj),
        in_specs=[
            pl.BlockSpec((kcol, n, LANE), lambda g, b, j: (g, b, 0)),
            pl.BlockSpec((1, cg, tn), lambda g, b, j: (g, 0, j)),
            pl.BlockSpec((ncol, n, LANE), lambda g, b, j: (zoff + g * nj + j, b, 0)),
            pl.BlockSpec((1, tn), lambda g, b, j: (0, g * nj + j)),
        ],
        out_specs=pl.BlockSpec((n, tn), lambda g, b, j: (b, g * nj + j)),
        out_shape=jax.ShapeDtypeStruct((nb * n, ng * cg), BF16),
        scratch_shapes=[pltpu.VMEM((n, cg), BF16)],
        compiler_params=_params(("parallel", "parallel", "arbitrary")),
        name="pool_mixer",
    )(p, w_grp, p, scale.reshape(1, ng * cg))


HGRN_LEVELS = (128, 64, 32, 16)
HGRN_DIAG = 8


def _hgrn_tri():
    t = np.arange(SLAB)
    fwd = (t[None, :] <= t[:, None]).astype(np.float32)
    tri = np.stack([np.concatenate([fwd, fwd], axis=1), np.concatenate([fwd.T, fwd.T], axis=1)])
    return jnp.asarray(tri, BF16)


def _hgrn_kernel(q_ref, af_ref, ab_ref, v_ref, z_ref, qc_ref, afc_ref, abc_ref, vc_ref, zc_ref,
                 lb_ref, gn_ref, tri_ref, u_ref, uc_ref,
                 o_scr, qh_scr, kb_scr, vt_scr, g_scr, qs_scr, kk_scr, b_scr, v_scr, ds_scr, sin_scr,
                 *, layer, t, m):
    nslab_c = m // SLAB
    nslab = (m + t) // SLAB

    lbs = []
    for d in (0, 1):
        raw = lb_ref[d]
        e = jnp.exp(raw - jnp.max(raw, axis=0, keepdims=True))
        probs = e / jnp.sum(e, axis=0, keepdims=True)
        lbs.append(jnp.sum(probs[1:layer + 1], axis=0, keepdims=True) if layer >= 1
                   else jnp.zeros((1, LANE), F32))

    ti = lax.broadcasted_iota(jnp.int32, (SLAB, SLAB), 0)
    si = lax.broadcasted_iota(jnp.int32, (SLAB, SLAB), 1)
    rt = lax.broadcasted_iota(jnp.int32, (SLAB, LANE), 0)
    diag_masks, level_masks = [], []
    for d in (0, 1):
        before = (si < ti) if d == 0 else (si > ti)
        diag_masks.append(((ti // HGRN_DIAG) == (si // HGRN_DIAG)) & (before | (si == ti)))
        level_masks.append({blk: ((ti // blk) == (si // blk)) & ((ti // (blk // 2)) != (si // (blk // 2)))
                            & before for blk in HGRN_LEVELS})

    def ref_delta(b, block, ref_row):
        b3 = b.reshape(SLAB // block, block, LANE)
        return (b3 - b3[:, ref_row:ref_row + 1, :]).reshape(SLAB, LANE)

    def gates_slab(srcs, r0, slab):
        row = slab * SLAB
        qs = _silu(srcs[0][0, pl.ds(r0, SLAB), :].astype(F32))
        qs_scr[pl.ds(row, SLAB), :] = qs
        v = srcs[3][0, pl.ds(r0, SLAB), :]
        v_scr[pl.ds(row, SLAB), :] = v
        vt_scr[slab] = v.astype(F32).T.astype(BF16)
        for d in (0, 1):
            a = srcs[1 + d][0, pl.ds(r0, SLAB), :].astype(F32)
            f = lbs[d] + (1.0 - lbs[d]) * (1.0 / (1.0 + jnp.exp(-a)))
            lf = jnp.log2(f)
            kk = 1.0 - f
            hi = lf.astype(BF16)
            lo = (lf - hi.astype(F32)).astype(BF16)
            b = _dot(tri_ref[d], jnp.concatenate([hi, lo], axis=0))
            b_end = b[SLAB - 1:SLAB] if d == 0 else b[0:1]
            kk_scr[d, pl.ds(row, SLAB), :] = kk
            b_scr[d, pl.ds(row, SLAB), :] = b
            qh_scr[d, pl.ds(row, SLAB), :] = (qs * jnp.exp2(b)).astype(BF16)
            kb_scr[d, pl.ds(row, SLAB), :] = (kk * jnp.exp2(b_end - b)).astype(BF16)
            g_scr[d, pl.ds(slab, 1), :] = jnp.exp2(b_end)

    def mix_slab(slab):
        row = pl.multiple_of(slab * SLAB, SLAB)
        qs = qs_scr[pl.ds(row, SLAB), :]
        v = v_scr[pl.ds(row, SLAB), :]
        atts = []
        for d in (0, 1):
            kk = kk_scr[d, pl.ds(row, SLAB), :]
            b = b_scr[d, pl.ds(row, SLAB), :]
            dlt = ref_delta(b, HGRN_DIAG, HGRN_DIAG // 2 - 1 if d == 0 else HGRN_DIAG // 2)
            qt = (qs * jnp.exp2(dlt)).astype(BF16)
            kt = (kk * jnp.exp2(-dlt)).astype(BF16)
            att = jnp.where(diag_masks[d], _nt_dot(qt, kt), 0.0)
            for block in HGRN_LEVELS:
                half = block // 2
                dlt = ref_delta(b, block, half - 1 if d == 0 else half)
                pos = rt & (block - 1)
                is_target = (pos >= half) if d == 0 else (pos < half)
                x = (jnp.where(is_target, qs, kk) * jnp.exp2(-jnp.abs(dlt))).astype(BF16)
                att = jnp.where(level_masks[d][block], _nt_dot(x, x), att)
            atts.append(att.astype(BF16))
        o = _dot(jnp.concatenate(atts, axis=1), jnp.concatenate([v, v], axis=0))
        for d in (0, 1):
            o = o + _nt_dot(qh_scr[d, pl.ds(row, SLAB), :], sin_scr[d, slab])
        o_scr[pl.ds(row, SLAB), :] = o

    csrc = (qc_ref, afc_ref, abc_ref, vc_ref)
    lsrc = (q_ref, af_ref, ab_ref, v_ref)
    for s in range(nslab_c):
        gates_slab(csrc, s * SLAB, s)

    def gates_body(s, carry):
        gates_slab(lsrc, pl.multiple_of(s * SLAB, SLAB), s + nslab_c)
        return carry

    lax.fori_loop(0, t // SLAB, gates_body, 0, unroll=8)

    def ds_body(s, carry):
        row = pl.multiple_of(s * SLAB, SLAB)
        for d in (0, 1):
            ds_scr[d, s] = _dot(vt_scr[s], kb_scr[d, pl.ds(row, SLAB), :])
        return carry

    lax.fori_loop(0, nslab, ds_body, 0, unroll=6)

    orders = (list(range(nslab)),
              list(range(nslab_c - 1, -1, -1)) + list(range(nslab - 1, nslab_c - 1, -1)))
    for d in (0, 1):
        st = jnp.zeros((LANE, LANE), F32)
        for slab in orders[d]:
            sin_scr[d, slab] = st.astype(BF16)
            st = g_scr[d, slab:slab + 1, :] * st + ds_scr[d, slab]

    def mix_body(s, carry):
        mix_slab(s)
        return carry

    lax.fori_loop(0, nslab, mix_body, 0, unroll=9)

    def finish(r0, n, z_blk, out_ref):
        o = o_scr[r0:r0 + n, :]
        ms = jnp.mean(o * o, axis=-1, keepdims=True)
        on = o * lax.rsqrt(ms + EPS) * gn_ref[...]
        out_ref[...] = (on * _silu(z_blk.astype(F32))).astype(BF16)

    finish(0, m, zc_ref[0], uc_ref)
    finish(m, t, z_ref[0], u_ref)


def _hgrn_call(p, pc, lb_raw, g_norm, layer, nb, t, m, nh):
    depth = lb_raw.shape[1]
    tt = m + t
    assert t % (8 * SLAB) == 0 and m % SLAB == 0 and (tt // SLAB) % 18 == 0
    lat = lambda k: pl.BlockSpec((1, t, LANE), lambda b, h, k=k: (k * nh + h, b, 0))
    ctx = lambda k: pl.BlockSpec((1, m, LANE), lambda b, h, k=k: (k * nh + h, b, 0))
    return pl.pallas_call(
        functools.partial(_hgrn_kernel, layer=layer, t=t, m=m),
        grid=(nb, nh),
        in_specs=[lat(k) for k in range(5)] + [ctx(k) for k in range(5)] + [
            pl.BlockSpec((2, depth, LANE), lambda b, h: (0, 0, h)),
            pl.BlockSpec((1, LANE), lambda b, h: (0, h)),
            pl.BlockSpec((2, SLAB, 2 * SLAB), lambda b, h: (0, 0, 0)),
        ],
        out_specs=[
            pl.BlockSpec((t, LANE), lambda b, h: (b, h)),
            pl.BlockSpec((m, LANE), lambda b, h: (b, h)),
        ],
        out_shape=[
            jax.ShapeDtypeStruct((nb * t, nh * LANE), BF16),
            jax.ShapeDtypeStruct((nb * m, nh * LANE), BF16),
        ],
        scratch_shapes=[
            pltpu.VMEM((tt, LANE), F32),
            pltpu.VMEM((2, tt, LANE), BF16),
            pltpu.VMEM((2, tt, LANE), BF16),
            pltpu.VMEM((tt // SLAB, LANE, SLAB), BF16),
            pltpu.VMEM((2, tt // SLAB, LANE), F32),
            pltpu.VMEM((tt, LANE), F32),
            pltpu.VMEM((2, tt, LANE), F32),
            pltpu.VMEM((2, tt, LANE), F32),
            pltpu.VMEM((tt, LANE), BF16),
            pltpu.VMEM((2, tt // SLAB, LANE, LANE), F32),
            pltpu.VMEM((2, tt // SLAB, LANE, LANE), BF16),
        ],
        compiler_params=_params(("parallel", "parallel")),
        name="hgrn_scan",
    )(p, p, p, p, p, pc, pc, pc, pc, pc, lb_raw, g_norm.reshape(1, nh * LANE), _hgrn_tri())


def kernel(x, c, ctx, c_ctx, w_mod, b_mod, g_pre, g_post, na_w_in, na_rpb, na_w_out,
           pool_w_in, pool_w_grp, pool_scale, pool_w_out, hgrn_w_in, hgrn_lb, hgrn_gnorm, hgrn_w_out):
    nb, t, d = x.shape
    m = ctx.shape[1]
    depth = w_mod.shape[0]
    nh = na_w_out.shape[1] // LANE
    rows = t // GRID_W

    mods = _mod_call(c, c_ctx, w_mod, b_mod)
    w_ins = (na_w_in, pool_w_in, hgrn_w_in)
    w_outs = tuple(w.astype(BF16) for w in (na_w_out, pool_w_out, hgrn_w_out))
    xl = x.reshape(nb * t, d)
    xc = ctx.reshape(nb * m, d)

    for i in range(depth):
        kind, j = i % 3, i // 3
        need_ctx = i < depth - 1
        mod = mods[i]
        shift, scale, gate = (mod[:nb, None, k * d:(k + 1) * d] for k in range(3))
        shift_c, scale_c, gate_c = (mod[nb:nb + 1, None, k * d:(k + 1) * d] for k in range(3))
        w_in, w_out = w_ins[kind], w_outs[kind]

        p = _proj_call(xl, g_pre[i], scale, shift, w_in, j, t, f"proj_lat_{i}")
        if kind == 0 and not need_ctx:
            pc = _proj_call(xc, g_pre[i], scale_c, shift_c, w_in, j, nb * m, f"proj_ctx_{i}",
                            col0=nh * LANE, ncols=2 * nh * LANE)
        else:
            pc = _proj_call(xc, g_pre[i], scale_c, shift_c, w_in, j, nb * m, f"proj_ctx_{i}")

        uc = None
        if kind == 0:
            u = _na_call(p, pc, na_rpb[j], nb, t, m, nh, ctx_first=0 if need_ctx else 1)
            if need_ctx:
                uc = _ctx_attn_call(pc, nb, m, nh)
        elif kind == 1:
            wg = pool_w_grp[j].astype(BF16)
            u = _pool_call(p, wg, pool_scale[j], nb, t)
            if need_ctx:
                uc = _pool_call(pc, wg, pool_scale[j], nb, m)
        else:
            u, uc = _hgrn_call(p, pc, hgrn_lb, hgrn_gnorm[j], i, nb, t, m, nh)

        xl = _out_call(u, w_out, j, xl, gate, g_post[i], t, f"out_lat_{i}")
        if need_ctx:
            xc = _out_call(uc, w_out, j, xc, gate_c, g_post[i], nb * m, f"out_ctx_{i}")

    return xl.reshape(nb, t, d)
```

```python
import functools

import numpy as np
import jax
import jax.numpy as jnp
from jax import lax
from jax.experimental import pallas as pl
from jax.experimental.pallas import tpu as pltpu

F32 = jnp.float32
BF16 = jnp.bfloat16

LANE = 128
VMEM_LIMIT = 56 * 1024 * 1024

EPS = 1e-6
GRID_W = 64
WIN_H = 8
WIN_W = 16
POOL_SIZES = (2, 4, 8, 16)
NEG = -1e30

NA_QROWS = 4
NA_HEADS = 4
LOG2E = 1.4426950408889634
NA_BAND = NA_QROWS + WIN_H - 1
SLAB = 128
PROJ_TM, PROJ_TN = 1024, 1024
OUT_TM = 512


def _silu(x):
    return x / (1.0 + jnp.exp(-x))


def _nt_dot(a, b):
    return lax.dot_general(a, b, (((1,), (1,)), ((), ())), preferred_element_type=F32)


def _dot(a, b):
    return jnp.dot(a, b, preferred_element_type=F32)


def _params(sem, vmem=VMEM_LIMIT):
    return pltpu.CompilerParams(dimension_semantics=sem, vmem_limit_bytes=vmem)


def _mod_kernel(c_ref, w_ref, b_ref, o_ref):
    s = _silu(c_ref[...])
    s_hi = s.astype(BF16)
    s_lo = (s - s_hi.astype(F32)).astype(BF16)
    w = w_ref[0]
    w_hi = w.astype(BF16)
    w_lo = (w - w_hi.astype(F32)).astype(BF16)
    acc = _dot(s_hi, w_hi) + _dot(s_lo, w_hi) + _dot(s_hi, w_lo)
    o_ref[0] = acc + b_ref[0]


def _mod_call(c, c_ctx, w_mod, b_mod):
    depth, d, n3 = w_mod.shape
    nb = c.shape[0]
    assert nb + 1 <= 8
    cs = jnp.zeros((8, d), F32).at[:nb].set(c).at[nb].set(c_ctx)
    tn = 1024 if n3 % 1024 == 0 else n3
    return pl.pallas_call(
        _mod_kernel,
        grid=(depth, n3 // tn),
        in_specs=[
            pl.BlockSpec((8, d), lambda l, j: (0, 0)),
            pl.BlockSpec((1, d, tn), lambda l, j: (l, 0, j)),
            pl.BlockSpec((1, 1, tn), lambda l, j: (l, 0, j)),
        ],
        out_specs=pl.BlockSpec((1, 8, tn), lambda l, j: (l, 0, j)),
        out_shape=jax.ShapeDtypeStruct((depth, 8, n3), F32),
        compiler_params=_params(("parallel", "parallel")),
        name="mod_vectors",
    )(cs, w_mod, b_mod.reshape(depth, 1, n3))


def _proj_kernel(x_ref, g_ref, sc_ref, sh_ref, w_ref, o_ref, h_ref, *, ncol):
    @pl.when(pl.program_id(1) == 0)
    def _():
        x = x_ref[...]
        ms = jnp.mean(x * x, axis=-1, keepdims=True)
        y = x * lax.rsqrt(ms + EPS) * g_ref[...]
        h_ref[...] = (y * (1.0 + sc_ref[0]) + sh_ref[0]).astype(BF16)

    acc = _dot(h_ref[...], w_ref[0].astype(BF16))
    for c in range(ncol):
        o_ref[c] = acc[:, c * LANE:(c + 1) * LANE].astype(BF16)


def _proj_call(x2d, g, scale, shift, w, layer, rows_per_batch, name, col0=0, ncols=None):
    r, d = x2d.shape
    n = w.shape[2] if ncols is None else ncols
    tm = min(PROJ_TM, rows_per_batch)
    tn = PROJ_TN if n % PROJ_TN == 0 and col0 % PROJ_TN == 0 else LANE
    assert r % tm == 0 and rows_per_batch % tm == 0 and n % tn == 0 and col0 % tn == 0
    ncol, j0 = tn // LANE, col0 // tn
    bidx = lambda i, j: ((i * tm) // rows_per_batch, 0, 0)
    return pl.pallas_call(
        functools.partial(_proj_kernel, ncol=ncol),
        grid=(r // tm, n // tn),
        in_specs=[
            pl.BlockSpec((tm, d), lambda i, j: (i, 0)),
            pl.BlockSpec((1, d), lambda i, j: (0, 0)),
            pl.BlockSpec((1, 1, d), bidx),
            pl.BlockSpec((1, 1, d), bidx),
            pl.BlockSpec((1, d, tn), lambda i, j: (layer, 0, j0 + j)),
        ],
        out_specs=pl.BlockSpec((ncol, tm, LANE), lambda i, j: (j, i, 0)),
        out_shape=jax.ShapeDtypeStruct((n // LANE, r, LANE), BF16),
        scratch_shapes=[pltpu.VMEM((tm, d), BF16)],
        compiler_params=_params(("parallel", "arbitrary")),
        name=name,
    )(x2d, g.reshape(1, d), scale, shift, w)


def _out_kernel(u_ref, w_ref, x_ref, gate_ref, gp_ref, o_ref):
    y = _dot(u_ref[...], w_ref[0])
    ms = jnp.mean(y * y, axis=-1, keepdims=True)
    r = y * lax.rsqrt(ms + EPS) * gp_ref[...]
    o_ref[...] = x_ref[...] + gate_ref[0] * r


def _out_call(u, w, layer, x2d, gate, g_post, rows_per_batch, name):
    r, kdim = u.shape
    d = w.shape[2]
    tm = min(OUT_TM, rows_per_batch)
    assert r % tm == 0 and rows_per_batch % tm == 0
    return pl.pallas_call(
        _out_kernel,
        grid=(r // tm,),
        in_specs=[
            pl.BlockSpec((tm, kdim), lambda i: (i, 0)),
            pl.BlockSpec((1, kdim, d), lambda i: (layer, 0, 0), pipeline_mode=pl.Buffered(1)),
            pl.BlockSpec((tm, d), lambda i: (i, 0)),
            pl.BlockSpec((1, 1, d), lambda i: ((i * tm) // rows_per_batch, 0, 0)),
            pl.BlockSpec((1, d), lambda i: (0, 0)),
        ],
        out_specs=pl.BlockSpec((tm, d), lambda i: (i, 0)),
        out_shape=jax.ShapeDtypeStruct((r, d), F32),
        compiler_params=_params(("parallel",)),
        name=name,
    )(u, w, x2d, gate, g_post.reshape(1, d))


def _na_block_plan(rows):
    plans = []
    for qs, bs in ((0, 0), (NA_QROWS, 0), (rows - NA_QROWS, rows - NA_BAND)):
        rq = qs + np.arange(NA_QROWS)
        rk = bs + np.arange(NA_BAND)
        r0 = np.clip(rq - WIN_H // 2, 0, rows - WIN_H)
        row_in = (rk[None, :] >= r0[:, None]) & (rk[None, :] < r0[:, None] + WIN_H)
        plans.append(np.where(row_in, rk[None, :] - rq[:, None] + (WIN_H - 1), -1))
    return plans


def _na_build_bias(rpb_ref, hh, tb_scr, bias_scr, rows):
    nrel_w = 2 * WIN_W - 1
    lane = lax.broadcasted_iota(jnp.int32, (GRID_W, LANE), 1)
    cq = lax.broadcasted_iota(jnp.int32, (GRID_W, LANE), 0)
    ck = lane & (GRID_W - 1)
    didx = ck - cq + (WIN_W - 1)
    c0 = jnp.clip(cq - WIN_W // 2, 0, GRID_W - WIN_W)
    col_in = (ck >= c0) & (ck < c0 + WIN_W)
    for dr in range(2 * WIN_H - 1):
        tile = jnp.zeros((GRID_W, LANE), F32)
        for j in range(nrel_w):
            tile = jnp.where(didx == j, rpb_ref[0, hh, dr * nrel_w + j] * LOG2E, tile)
        tb_scr[hh, dr] = jnp.where(col_in, tile, NEG)
    neg = jnp.full((GRID_W, LANE), NEG, F32)
    for var, plan in enumerate(_na_block_plan(rows)):
        for rq in range(NA_QROWS):
            for pair in range((NA_BAND + 1) // 2):
                rks = [rk for rk in (2 * pair, 2 * pair + 1) if rk < NA_BAND]
                src = [tb_scr[hh, int(plan[rq, rk])] if plan[rq, rk] >= 0 else neg for rk in rks]
                tile = src[0] if len(src) == 1 else jnp.where(lane < GRID_W, src[0], src[1])
                width = GRID_W * len(rks)
                bias_scr[hh, var, rq * GRID_W:(rq + 1) * GRID_W,
                         pair * LANE:pair * LANE + width] = tile[:, :width]


def _na_kernel(rpb_ref, q_ref, k_ref, v_ref, z_ref, kc_ref, vc_ref, o_ref, tb_scr, bias_scr,
               *, rows, heads):
    nqb = rows // NA_QROWS
    tq = NA_QROWS * GRID_W
    nkeys = NA_BAND * GRID_W

    @pl.when(pl.program_id(1) == 0)
    def _():
        for hh in range(heads):
            _na_build_bias(rpb_ref, hh, tb_scr, bias_scr, rows)

    def query_block(qb, carry):
        var = jnp.where(qb == 0, 0, jnp.where(qb == nqb - 1, 2, 1))
        rs = jnp.clip(qb * NA_QROWS - WIN_H // 2, 0, rows - NA_BAND)
        start = pl.multiple_of(rs * GRID_W, GRID_W)
        row = pl.multiple_of(qb * tq, tq)
        for hh in range(heads):
            q = (q_ref[hh, pl.ds(row, tq), :].astype(F32) * (LANE ** -0.5 * LOG2E)).astype(BF16)
            kb = k_ref[hh, pl.ds(start, nkeys), :]
            vb = v_ref[hh, pl.ds(start, nkeys), :]
            s1 = _nt_dot(q, kb) + bias_scr[hh, var]
            s2 = _nt_dot(q, kc_ref[hh])
            mx = jnp.maximum(jnp.max(s1, axis=-1, keepdims=True), jnp.max(s2, axis=-1, keepdims=True))
            p1 = jnp.exp2(s1 - mx)
            p2 = jnp.exp2(s2 - mx)
            l = jnp.sum(p1, axis=-1, keepdims=True) + jnp.sum(p2, axis=-1, keepdims=True)
            o = (_dot(p1.astype(BF16), vb) + _dot(p2.astype(BF16), vc_ref[hh])) / l
            z = z_ref[hh, pl.ds(row, tq), :].astype(F32)
            o_ref[pl.ds(row, tq), hh * LANE:(hh + 1) * LANE] = (o * _silu(z)).astype(BF16)
        return carry

    lax.fori_loop(0, nqb, query_block, 0, unroll=4)


def _na_call(p, pc, rpb, nb, t, m, nh, ctx_first=0):
    rows = t // GRID_W
    nq, nk = NA_QROWS * GRID_W, NA_BAND * GRID_W
    nrel = rpb.shape[1] * rpb.shape[2]
    hb = NA_HEADS if nh % NA_HEADS == 0 else 1
    ng = nh // hb
    lat = lambda k: pl.BlockSpec((hb, t, LANE), lambda h, b, k=k: (k * ng + h, b, 0))
    ctx = lambda k: pl.BlockSpec((hb, m, LANE), lambda h, b, k=k: ((k - ctx_first) * ng + h, b, 0))
    return pl.pallas_call(
        functools.partial(_na_kernel, rows=rows, heads=hb),
        grid=(ng, nb),
        in_specs=[
            pl.BlockSpec((1, hb, nrel), lambda h, b: (h, 0, 0), memory_space=pltpu.SMEM),
            lat(0), lat(1), lat(2), lat(3), ctx(1), ctx(2),
        ],
        out_specs=pl.BlockSpec((t, hb * LANE), lambda h, b: (b, h)),
        out_shape=jax.ShapeDtypeStruct((nb * t, nh * LANE), BF16),
        scratch_shapes=[
            pltpu.VMEM((hb, 2 * WIN_H - 1, GRID_W, LANE), F32),
            pltpu.VMEM((hb, 3, nq, nk), F32),
        ],
        compiler_params=_params(("parallel", "arbitrary")),
        name="na_attention",
    )(rpb.reshape(ng, hb, nrel), p, p, p, p, pc, pc)


def _ctx_attn_kernel(q_ref, k_ref, v_ref, z_ref, o_ref, *, heads):
    for hh in range(heads):
        q = (q_ref[hh].astype(F32) * (LANE ** -0.5)).astype(BF16)
        s = _nt_dot(q, k_ref[hh])
        mx = jnp.max(s, axis=-1, keepdims=True)
        p = jnp.exp(s - mx)
        l = jnp.sum(p, axis=-1, keepdims=True)
        o = _dot(p.astype(BF16), v_ref[hh]) / l
        o_ref[:, hh * LANE:(hh + 1) * LANE] = (o * _silu(z_ref[hh].astype(F32))).astype(BF16)


def _ctx_attn_call(pc, nb, m, nh):
    hb = NA_HEADS if nh % NA_HEADS == 0 else 1
    ng = nh // hb
    blk = lambda k: pl.BlockSpec((hb, m, LANE), lambda h, b, k=k: (k * ng + h, b, 0))
    return pl.pallas_call(
        functools.partial(_ctx_attn_kernel, heads=hb),
        grid=(ng, nb),
        in_specs=[blk(0), blk(1), blk(2), blk(3)],
        out_specs=pl.BlockSpec((m, hb * LANE), lambda h, b: (b, h)),
        out_shape=jax.ShapeDtypeStruct((nb * m, nh * LANE), BF16),
        compiler_params=_params(("parallel", "parallel")),
        name="ctx_attention",
    )(pc, pc, pc, pc)


def _pool_kernel(u_ref, w_ref, z_ref, s_ref, o_ref, lhs_scr, *, n, kcol, ncol):
    group = pl.program_id(0)
    t = lax.broadcasted_iota(jnp.int32, (n, LANE), 0)

    def up(a, k):
        return jnp.where(t < n - k, pltpu.roll(a, n - k, axis=0), 0.0)

    def down(a, k):
        return jnp.where(t >= k, pltpu.roll(a, k, axis=0), 0.0)

    for gi, w in enumerate(POOL_SIZES):
        half = w // 2

        @pl.when((group == gi) & (pl.program_id(2) == 0))
        def _(half=half):
            cnt = (jnp.minimum(t + half, n) - jnp.maximum(t - half, 0)).astype(F32)
            for c in range(kcol):
                x = u_ref[c].astype(F32)
                fwd, bwd, k = x, x, 1
                while k < half:
                    fwd = fwd + up(fwd, k)
                    bwd = bwd + down(bwd, k)
                    k *= 2
                total = fwd + down(bwd, 1)
                lhs_scr[:, c * LANE:(c + 1) * LANE] = (total / cnt - x).astype(BF16)

    acc = _dot(lhs_scr[...], w_ref[0])
    z = jnp.concatenate([z_ref[c] for c in range(ncol)], axis=-1).astype(F32)
    o_ref[...] = (acc * s_ref[...] * _silu(z)).astype(BF16)


def _pool_call(p, w_grp, scale, nb, n):
    ng, cg, _ = w_grp.shape
    assert ng == len(POOL_SIZES)
    tn = min(512, cg)
    kcol, ncol, nj = cg // LANE, tn // LANE, cg // tn
    zoff = (ng * cg) // tn
    return pl.pallas_call(
        functools.partial(_pool_kernel, n=n, kcol=kcol, ncol=ncol),
        grid=(ng, nb, nj),
        in_specs=[
            pl.BlockSpec((kcol, n, LANE), lambda g, b, j: (g, b, 0)),
            pl.BlockSpec((1, cg, tn), lambda g, b, j: (g, 0, j)),
            pl.BlockSpec((ncol, n, LANE), lambda g, b, j: (zoff + g * nj + j, b, 0)),
            pl.BlockSpec((1, tn), lambda g, b, j: (0, g * nj + j)),
        ],
        out_specs=pl.BlockSpec((n, tn), lambda g, b, j: (b, g * nj + j)),
        out_shape=jax.ShapeDtypeStruct((nb * n, ng * cg), BF16),
        scratch_shapes=[pltpu.VMEM((n, cg), BF16)],
        compiler_params=_params(("parallel", "parallel", "arbitrary")),
        name="pool_mixer",
    )(p, w_grp, p, scale.reshape(1, ng * cg))


HGRN_LEVELS = (128, 64, 32, 16)
HGRN_DIAG = 8


def _hgrn_tri():
    t = np.arange(SLAB)
    fwd = (t[None, :] <= t[:, None]).astype(np.float32)
    tri = np.stack([np.concatenate([fwd, fwd], axis=1), np.concatenate([fwd.T, fwd.T], axis=1)])
    return jnp.asarray(tri, BF16)


def _hgrn_kernel(q_ref, af_ref, ab_ref, v_ref, z_ref, qc_ref, afc_ref, abc_ref, vc_ref, zc_ref,
                 lb_ref, gn_ref, tri_ref, u_ref, uc_ref,
                 o_scr, qh_scr, kb_scr, vt_scr, g_scr, qs_scr, kk_scr, b_scr, v_scr, ds_scr, sin_scr,
                 *, layer, t, m):
    nslab_c = m // SLAB
    nslab = (m + t) // SLAB

    lbs = []
    for d in (0, 1):
        raw = lb_ref[d]
        e = jnp.exp(raw - jnp.max(raw, axis=0, keepdims=True))
        probs = e / jnp.sum(e, axis=0, keepdims=True)
        lbs.append(jnp.sum(probs[1:layer + 1], axis=0, keepdims=True) if layer >= 1
                   else jnp.zeros((1, LANE), F32))

    ti = lax.broadcasted_iota(jnp.int32, (SLAB, SLAB), 0)
    si = lax.broadcasted_iota(jnp.int32, (SLAB, SLAB), 1)
    rt = lax.broadcasted_iota(jnp.int32, (SLAB, LANE), 0)
    diag_masks, level_masks = [], []
    for d in (0, 1):
        before = (si < ti) if d == 0 else (si > ti)
        diag_masks.append(((ti // HGRN_DIAG) == (si // HGRN_DIAG)) & (before | (si == ti)))
        level_masks.append({blk: ((ti // blk) == (si // blk)) & ((ti // (blk // 2)) != (si // (blk // 2)))
                            & before for blk in HGRN_LEVELS})

    def ref_delta(b, block, ref_row):
        b3 = b.reshape(SLAB // block, block, LANE)
        return (b3 - b3[:, ref_row:ref_row + 1, :]).reshape(SLAB, LANE)

    def gates_slab(srcs, r0, slab):
        row = slab * SLAB
        qs = _silu(srcs[0][0, pl.ds(r0, SLAB), :].astype(F32))
        qs_scr[pl.ds(row, SLAB), :] = qs
        v = srcs[3][0, pl.ds(r0, SLAB), :]
        v_scr[pl.ds(row, SLAB), :] = v
        vt_scr[slab] = v.astype(F32).T.astype(BF16)
        for d in (0, 1):
            a = srcs[1 + d][0, pl.ds(r0, SLAB), :].astype(F32)
            f = lbs[d] + (1.0 - lbs[d]) * (1.0 / (1.0 + jnp.exp(-a)))
            lf = jnp.log2(f)
            kk = 1.0 - f
            hi = lf.astype(BF16)
            lo = (lf - hi.astype(F32)).astype(BF16)
            b = _dot(tri_ref[d], jnp.concatenate([hi, lo], axis=0))
            b_end = b[SLAB - 1:SLAB] if d == 0 else b[0:1]
            kk_scr[d, pl.ds(row, SLAB), :] = kk
            b_scr[d, pl.ds(row, SLAB), :] = b
            qh_scr[d, pl.ds(row, SLAB), :] = (qs * jnp.exp2(b)).astype(BF16)
            kb_scr[d, pl.ds(row, SLAB), :] = (kk * jnp.exp2(b_end - b)).astype(BF16)
            g_scr[d, pl.ds(slab, 1), :] = jnp.exp2(b_end)

    def mix_slab(slab):
        row = pl.multiple_of(slab * SLAB, SLAB)
        qs = qs_scr[pl.ds(row, SLAB), :]
        v = v_scr[pl.ds(row, SLAB), :]
        atts = []
        for d in (0, 1):
            kk = kk_scr[d, pl.ds(row, SLAB), :]
            b = b_scr[d, pl.ds(row, SLAB), :]
            dlt = ref_delta(b, HGRN_DIAG, HGRN_DIAG // 2 - 1 if d == 0 else HGRN_DIAG // 2)
            qt = (qs * jnp.exp2(dlt)).astype(BF16)
            kt = (kk * jnp.exp2(-dlt)).astype(BF16)
            att = jnp.where(diag_masks[d], _nt_dot(qt, kt), 0.0)
            for block in HGRN_LEVELS:
                half = block // 2
                dlt = ref_delta(b, block, half - 1 if d == 0 else half)
                pos = rt & (block - 1)
                is_target = (pos >= half) if d == 0 else (pos < half)
                x = (jnp.where(is_target, qs, kk) * jnp.exp2(-jnp.abs(dlt))).astype(BF16)
                att = jnp.where(level_masks[d][block], _nt_dot(x, x), att)
            atts.append(att.astype(BF16))
        o = _dot(jnp.concatenate(atts, axis=1), jnp.concatenate([v, v], axis=0))
        for d in (0, 1):
            o = o + _nt_dot(qh_scr[d, pl.ds(row, SLAB), :], sin_scr[d, slab])
        o_scr[pl.ds(row, SLAB), :] = o

    csrc = (qc_ref, afc_ref, abc_ref, vc_ref)
    lsrc = (q_ref, af_ref, ab_ref, v_ref)
    for s in range(nslab_c):
        gates_slab(csrc, s * SLAB, s)

    def gates_body(s, carry):
        gates_slab(lsrc, pl.multiple_of(s * SLAB, SLAB), s + nslab_c)
        return carry

    lax.fori_loop(0, t // SLAB, gates_body, 0, unroll=8)

    def ds_body(s, carry):
        row = pl.multiple_of(s * SLAB, SLAB)
        for d in (0, 1):
            ds_scr[d, s] = _dot(vt_scr[s], kb_scr[d, pl.ds(row, SLAB), :])
        return carry

    lax.fori_loop(0, nslab, ds_body, 0, unroll=6)

    orders = (list(range(nslab)),
              list(range(nslab_c - 1, -1, -1)) + list(range(nslab - 1, nslab_c - 1, -1)))
    for d in (0, 1):
        st = jnp.zeros((LANE, LANE), F32)
        for slab in orders[d]:
            sin_scr[d, slab] = st.astype(BF16)
            st = g_scr[d, slab:slab + 1, :] * st + ds_scr[d, slab]

    def mix_body(s, carry):
        mix_slab(s)
        return carry

    lax.fori_loop(0, nslab, mix_body, 0, unroll=18)

    def finish(r0, n, z_blk, out_ref):
        o = o_scr[r0:r0 + n, :]
        ms = jnp.mean(o * o, axis=-1, keepdims=True)
        on = o * lax.rsqrt(ms + EPS) * gn_ref[...]
        out_ref[...] = (on * _silu(z_blk.astype(F32))).astype(BF16)

    finish(0, m, zc_ref[0], uc_ref)
    finish(m, t, z_ref[0], u_ref)


def _hgrn_call(p, pc, lb_raw, g_norm, layer, nb, t, m, nh):
    depth = lb_raw.shape[1]
    tt = m + t
    assert t % (8 * SLAB) == 0 and m % SLAB == 0 and (tt // SLAB) % 18 == 0
    lat = lambda k: pl.BlockSpec((1, t, LANE), lambda b, h, k=k: (k * nh + h, b, 0))
    ctx = lambda k: pl.BlockSpec((1, m, LANE), lambda b, h, k=k: (k * nh + h, b, 0))
    return pl.pallas_call(
        functools.partial(_hgrn_kernel, layer=layer, t=t, m=m),
        grid=(nb, nh),
        in_specs=[lat(k) for k in range(5)] + [ctx(k) for k in range(5)] + [
            pl.BlockSpec((2, depth, LANE), lambda b, h: (0, 0, h)),
            pl.BlockSpec((1, LANE), lambda b, h: (0, h)),
            pl.BlockSpec((2, SLAB, 2 * SLAB), lambda b, h: (0, 0, 0)),
        ],
        out_specs=[
            pl.BlockSpec((t, LANE), lambda b, h: (b, h)),
            pl.BlockSpec((m, LANE), lambda b, h: (b, h)),
        ],
        out_shape=[
            jax.ShapeDtypeStruct((nb * t, nh * LANE), BF16),
            jax.ShapeDtypeStruct((nb * m, nh * LANE), BF16),
        ],
        scratch_shapes=[
            pltpu.VMEM((tt, LANE), F32),
            pltpu.VMEM((2, tt, LANE), BF16),
            pltpu.VMEM((2, tt, LANE), BF16),
            pltpu.VMEM((tt // SLAB, LANE, SLAB), BF16),
            pltpu.VMEM((2, tt // SLAB, LANE), F32),
            pltpu.VMEM((tt, LANE), F32),
            pltpu.VMEM((2, tt, LANE), F32),
            pltpu.VMEM((2, tt, LANE), F32),
            pltpu.VMEM((tt, LANE), BF16),
            pltpu.VMEM((2, tt // SLAB, LANE, LANE), F32),
            pltpu.VMEM((2, tt // SLAB, LANE, LANE), BF16),
        ],
        compiler_params=_params(("parallel", "parallel")),
        name="hgrn_scan",
    )(p, p, p, p, p, pc, pc, pc, pc, pc, lb_raw, g_norm.reshape(1, nh * LANE), _hgrn_tri())


def kernel(x, c, ctx, c_ctx, w_mod, b_mod, g_pre, g_post, na_w_in, na_rpb, na_w_out,
           pool_w_in, pool_w_grp, pool_scale, pool_w_out, hgrn_w_in, hgrn_lb, hgrn_gnorm, hgrn_w_out):
    nb, t, d = x.shape
    m = ctx.shape[1]
    depth = w_mod.shape[0]
    nh = na_w_out.shape[1] // LANE
    rows = t // GRID_W

    mods = _mod_call(c, c_ctx, w_mod, b_mod)
    w_ins = (na_w_in, pool_w_in, hgrn_w_in)
    w_outs = tuple(w.astype(BF16) for w in (na_w_out, pool_w_out, hgrn_w_out))
    xl = x.reshape(nb * t, d)
    xc = ctx.reshape(nb * m, d)

    for i in range(depth):
        kind, j = i % 3, i // 3
        need_ctx = i < depth - 1
        mod = mods[i]
        shift, scale, gate = (mod[:nb, None, k * d:(k + 1) * d] for k in range(3))
        shift_c, scale_c, gate_c = (mod[nb:nb + 1, None, k * d:(k + 1) * d] for k in range(3))
        w_in, w_out = w_ins[kind], w_outs[kind]

        p = _proj_call(xl, g_pre[i], scale, shift, w_in, j, t, f"proj_lat_{i}")
        if kind == 0 and not need_ctx:
            pc = _proj_call(xc, g_pre[i], scale_c, shift_c, w_in, j, nb * m, f"proj_ctx_{i}",
                            col0=nh * LANE, ncols=2 * nh * LANE)
        else:
            pc = _proj_call(xc, g_pre[i], scale_c, shift_c, w_in, j, nb * m, f"proj_ctx_{i}")

        uc = None
        if kind == 0:
            u = _na_call(p, pc, na_rpb[j], nb, t, m, nh, ctx_first=0 if need_ctx else 1)
            if need_ctx:
                uc = _ctx_attn_call(pc, nb, m, nh)
        elif kind == 1:
            wg = pool_w_grp[j].astype(BF16)
            u = _pool_call(p, wg, pool_scale[j], nb, t)
            if need_ctx:
                uc = _pool_call(pc, wg, pool_scale[j], nb, m)
        else:
            u, uc = _hgrn_call(p, pc, hgrn_lb, hgrn_gnorm[j], i, nb, t, m, nh)

        xl = _out_call(u, w_out, j, xl, gate, g_post[i], t, f"out_lat_{i}")
        if need_ctx:
            xc = _out_call(uc, w_out, j, xc, gate_c, g_post[i], nb * m, f"out_ctx_{i}")

    return xl.reshape(nb, t, d)
```
